```python
import math
import jax, jax.numpy as jnp
from jax import lax
import numpy as np

D_MODEL = 1024
BATCH = 2
SEQ = 8192
DEPTH = 4
DEC_BATCH = 32
DEC_SEQ = 4
PAST_LEN = 8192
PAGE_SIZE = 128

N_MIXERS = 3
N_A = len(range(0, DEPTH, N_MIXERS))
N_B = len(range(1, DEPTH, N_MIXERS))
N_C = len(range(2, DEPTH, N_MIXERS))
NORM_EPS = 1e-6
LN_EPS = 1e-5
DN_DK = 128
DN_DV = 128
DN_HK = D_MODEL // DN_DK
DN_HV = 2 * DN_HK
DN_CONV = 4
DN_CHUNK = 64
DN_QK = DN_HK * DN_DK
DN_VW = DN_HV * DN_DV
DN_CONV_DIM = 2 * DN_QK + DN_VW
DN_IN = DN_CONV_DIM + DN_VW + 2 * DN_HV
SG_WIDTH = D_MODEL
SG_CHUNK = 128
SG_GROUPS = 8
SG_GW = SG_WIDTH // SG_GROUPS
NSA_DH = 64
NSA_HQ = D_MODEL // NSA_DH
NSA_HKV = 4
NSA_G = NSA_HQ // NSA_HKV
CMP_STRIDE = 16
CMP_BLOCK = 2 * CMP_STRIDE
SLC_BLOCK = 64
N_SEL = 16
WINDOW = 512
Q_BLOCK = 128
FORCE_BONUS = 1e4
NSA_IN = NSA_HQ * NSA_DH + 6 * NSA_HKV * NSA_DH + 3 * NSA_HQ
ROPE_THETA = 10000.0
FFN_HIDDEN = ((8 * D_MODEL // 3 + 255) // 256) * 256

kernel_name = 'hybrid_deltanet_sgmlp_nsa_step'


def rms_norm(x, g, eps=NORM_EPS):
    xf = x.astype(jnp.float32)
    y = xf * lax.rsqrt(jnp.mean(xf * xf, -1, keepdims=True) + eps)
    return (y * g.astype(jnp.float32)).astype(x.dtype)


def layer_norm(x, g, b, eps=LN_EPS):
    xf = x.astype(jnp.float32)
    mu = jnp.mean(xf, -1, keepdims=True)
    xc = xf - mu
    y = xc * lax.rsqrt(jnp.mean(xc * xc, -1, keepdims=True) + eps)
    return (y * g.astype(jnp.float32) + b.astype(jnp.float32)).astype(x.dtype)


def l2_norm(x, eps=1e-6):
    return x * lax.rsqrt(jnp.sum(x * x, -1, keepdims=True) + eps)


def masked_softmax(s, mask):
    s = jnp.where(mask, s, -jnp.inf)
    m = jnp.max(s, -1, keepdims=True)
    m = jnp.where(jnp.isfinite(m), m, 0.0)
    p = jnp.where(mask, jnp.exp(s - m), 0.0)
    d = jnp.sum(p, -1, keepdims=True)
    return p / jnp.where(d > 0, d, 1.0)


def rope(x, pos):
    half = x.shape[-1] // 2
    inv = ROPE_THETA ** (-jnp.arange(half, dtype=jnp.float32) / half)
    ang = pos.astype(jnp.float32)[:, None] * inv[None, :]
    cos, sin = jnp.cos(ang)[None, :, None, :], jnp.sin(ang)[None, :, None, :]
    xf = x.astype(jnp.float32)
    x1, x2 = xf[..., :half], xf[..., half:]
    return jnp.concatenate([x1 * cos - x2 * sin, x2 * cos + x1 * sin], -1).astype(x.dtype)


def swiglu(x, wg, wu, wd):
    return (jax.nn.silu(x @ wg) * (x @ wu)) @ wd


def gated_delta_rule(q, k, v, g, beta, s0):
    B, L, H, _ = q.shape
    C = DN_CHUNK
    n = -(-L // C)
    pad = n * C - L

    def to_chunks(a):
        a = jnp.pad(a, [(0, 0), (0, pad)] + [(0, 0)] * (a.ndim - 2))
        return jnp.moveaxis(a.reshape((B, n, C) + a.shape[2:]), 3, 2)

    qc, kc, vc, gc, bc = [to_chunks(a) for a in (q, k, v, g, beta)]
    gcum = jnp.cumsum(gc, -1)
    tri = jnp.tril(jnp.ones((C, C), bool))
    stri = jnp.tril(jnp.ones((C, C), bool), -1)
    decay = jnp.exp(jnp.where(tri, gcum[..., :, None] - gcum[..., None, :], -jnp.inf))
    kb = kc * bc[..., None]
    a_mat = jnp.where(stri, jnp.einsum('bnhid,bnhjd->bnhij', kb, kc) * decay, 0.0) + jnp.eye(C, dtype=jnp.float32)
    rhs = jnp.concatenate([vc * bc[..., None], kb * jnp.exp(gcum)[..., None]], -1)
    sol = lax.linalg.triangular_solve(a_mat, rhs, left_side=True, lower=True)
    u, w = sol[..., :DN_DV], sol[..., DN_DV:]
    qk = jnp.where(tri, jnp.einsum('bnhid,bnhjd->bnhij', qc, kc) * decay, 0.0)
    qg = qc * jnp.exp(gcum)[..., None]
    kdec = kc * jnp.exp(gcum[..., -1:] - gcum)[..., None]
    glast = jnp.exp(gcum[..., -1])

    def step(S, xs):
        qk_i, qg_i, u_i, w_i, kdec_i, gl_i = xs
        v_new = u_i - jnp.einsum('bhck,bhkv->bhcv', w_i, S)
        o = jnp.einsum('bhck,bhkv->bhcv', qg_i, S) + jnp.einsum('bhij,bhjv->bhiv', qk_i, v_new)
        S = S * gl_i[..., None, None] + jnp.einsum('bhck,bhcv->bhkv', kdec_i, v_new)
        return S, o

    xs = tuple(jnp.moveaxis(a, 1, 0) for a in (qk, qg, u, w, kdec, glast))
    S, o = lax.scan(step, s0, xs)
    o = jnp.moveaxis(jnp.moveaxis(o, 0, 1), 2, 3).reshape(B, n * C, H, DN_DV)[:, :L]
    return o, S


def delta_mixer(x, conv_buf, s0, w_in, conv_w, a_log, dt_bias, norm_g, w_out):
    B, L, _ = x.shape
    f32 = jnp.float32
    proj = x @ w_in
    qkv, z, b_raw, a_raw = jnp.split(proj, [DN_CONV_DIM, DN_CONV_DIM + DN_VW, DN_CONV_DIM + DN_VW + DN_HV], axis=-1)
    full = jnp.concatenate([conv_buf.astype(x.dtype), qkv], axis=1)
    conv = sum(full[:, j:j + L] * conv_w[j] for j in range(DN_CONV))
    conv = jax.nn.silu(conv)
    new_buf = full[:, L:]
    q, k, v = jnp.split(conv, [DN_QK, 2 * DN_QK], axis=-1)
    rep = DN_HV // DN_HK
    q = jnp.repeat(l2_norm(q.reshape(B, L, DN_HK, DN_DK).astype(f32)) * DN_DK ** -0.5, rep, axis=2)
    k = jnp.repeat(l2_norm(k.reshape(B, L, DN_HK, DN_DK).astype(f32)), rep, axis=2)
    v = v.reshape(B, L, DN_HV, DN_DV).astype(f32)
    beta = jax.nn.sigmoid(b_raw.astype(f32))
    g = -jnp.exp(a_log.astype(f32)) * jax.nn.softplus(a_raw.astype(f32) + dt_bias.astype(f32))
    o, S = gated_delta_rule(q, k, v, g, beta, s0.astype(f32))
    o = rms_norm(o, norm_g) * jax.nn.silu(z.reshape(B, L, DN_HV, DN_DV).astype(f32))
    y = o.reshape(B, L, DN_VW).astype(x.dtype) @ w_out
    return y, S.astype(s0.dtype), new_buf


def spatial_gating_mixer(x, w_in, ln_g, ln_b, w_sp, b_sp, w_out):
    B, L, _ = x.shape
    u, v = jnp.split(jax.nn.gelu(x @ w_in), 2, axis=-1)
    v = layer_norm(v, ln_g, ln_b)
    n = -(-L // SG_CHUNK)
    vc = jnp.pad(v, ((0, 0), (0, n * SG_CHUNK - L), (0, 0))).reshape(B, n, SG_CHUNK, SG_GROUPS, SG_GW)
    mixed = jnp.einsum('gts,bnsgc->bntgc', jnp.tril(w_sp), vc) + b_sp.T[None, None, :, :, None]
    mixed = mixed.reshape(B, n * SG_CHUNK, SG_WIDTH)[:, :L]
    return (u * mixed) @ w_out, v


def nsa_project(x, pos, w_in, q_norm, k_norm):
    B, L, _ = x.shape
    qw, hd = NSA_HQ * NSA_DH, NSA_HKV * NSA_DH
    parts = jnp.split(x @ w_in, [qw + i * hd for i in range(7)], axis=-1)
    q, kc, vc, ks, vs, kw, vw, gate = parts
    heads = lambda a, h: a.reshape(B, L, h, NSA_DH)
    q = rms_norm(heads(q, NSA_HQ), q_norm)
    q_rot = rope(q, pos)
    ks = rope(rms_norm(heads(ks, NSA_HKV), k_norm[1]), pos)
    kw = rope(rms_norm(heads(kw, NSA_HKV), k_norm[2]), pos)
    gate = jax.nn.sigmoid(gate.astype(jnp.float32)).reshape(B, L, NSA_HQ, 3)
    return (q, q_rot, gate, heads(kc, NSA_HKV), heads(vc, NSA_HKV), ks, heads(vs, NSA_HKV), kw, heads(vw, NSA_HKV))


def compress_rows(rows, pe, w):
    B, T = rows.shape[:2]
    n_seg = -(-T // CMP_STRIDE)
    rows = jnp.pad(rows, ((0, 0), (0, n_seg * CMP_STRIDE - T), (0, 0), (0, 0)))
    seg = rows.reshape(B, n_seg, CMP_STRIDE, NSA_HKV, NSA_DH)
    w = w.reshape(CMP_BLOCK, NSA_DH, NSA_DH)
    pe_term = jnp.einsum('ld,lde->e', pe, w)
    return (jnp.einsum('bnlhd,lde->bnhe', seg[:, :-1], w[:CMP_STRIDE])
            + jnp.einsum('bnlhd,lde->bnhe', seg[:, 1:], w[CMP_STRIDE:]) + pe_term)


def nsa_attend(q, q_rot, gate, pos0, kc_all, vc_all, ks_all, vs_all, slab_fn, k_norm_cmp, pe_k, pe_v, w_ck, w_cv):
    f32 = jnp.float32
    B, L = q.shape[:2]
    T = kc_all.shape[1]
    kcmp = rms_norm(compress_rows(kc_all, pe_k, w_ck), k_norm_cmp).astype(f32)
    vcmp = compress_rows(vc_all, pe_v, w_cv).astype(f32)
    n_cmp = kcmp.shape[1]
    cmp_start = jnp.arange(n_cmp) * CMP_STRIDE
    cmp_end = cmp_start + CMP_BLOCK - 1
    n_slc = -(-T // SLC_BLOCK)
    slc_start = jnp.arange(n_slc) * SLC_BLOCK
    ov = (jnp.minimum(cmp_start[:, None] + CMP_BLOCK, slc_start[None, :] + SLC_BLOCK)
          - jnp.maximum(cmp_start[:, None], slc_start[None, :]))
    cmp_to_slc = jnp.clip(ov, 0, None).astype(f32) / CMP_BLOCK

    def to_blocks(a):
        a = jnp.pad(a, ((0, 0), (0, n_slc * SLC_BLOCK - T), (0, 0), (0, 0)))
        return a.reshape(B, n_slc, SLC_BLOCK, NSA_HKV, NSA_DH).transpose(0, 3, 1, 2, 4)

    ksb, vsb = to_blocks(ks_all), to_blocks(vs_all)
    k_sel = min(N_SEL, n_slc)
    qb = min(Q_BLOCK, L)
    nblk = L // qb
    gather = jax.vmap(jax.vmap(lambda blk, i: blk[i]))
    scale = NSA_DH ** -0.5
    blk_ids = jnp.arange(n_slc)
    m_sel = k_sel * SLC_BLOCK

    def one_block(args):
        bi, qg, qrg, gt = args
        qpos = pos0 + bi * qb + jnp.arange(qb)
        qg = qg.reshape(B, qb, NSA_HKV, NSA_G, NSA_DH).astype(f32)
        qrg = qrg.reshape(B, qb, NSA_HKV, NSA_G, NSA_DH).astype(f32)
        s = jnp.einsum('bqhgd,bkhd->bhgqk', qg, kcmp) * scale
        p_cmp = masked_softmax(s, cmp_end[None, :] <= qpos[:, None])
        o_cmp = jnp.einsum('bhgqk,bkhd->bqhgd', p_cmp, vcmp)
        imp = jnp.einsum('bhgqk,kj->bhqj', p_cmp, cmp_to_slc)
        cur = (qpos // SLC_BLOCK)[:, None]
        eligible = blk_ids[None, :] * SLC_BLOCK <= qpos[:, None]
        forced = (blk_ids[None, :] == 0) | (blk_ids[None, :] == cur) | (blk_ids[None, :] == cur - 1)
        score = jnp.where(eligible, imp + jnp.where(forced, FORCE_BONUS, 0.0), -jnp.inf)
        _, idx = lax.top_k(score, k_sel)
        k_g = gather(ksb, idx).astype(f32).reshape(B, NSA_HKV, qb, m_sel, NSA_DH)
        v_g = gather(vsb, idx).astype(f32).reshape(B, NSA_HKV, qb, m_sel, NSA_DH)
        kpos = idx[..., None] * SLC_BLOCK + jnp.arange(SLC_BLOCK)
        m_slc = (kpos <= qpos[None, None, :, None, None]).reshape(B, NSA_HKV, 1, qb, m_sel)
        s = jnp.einsum('bqhgd,bhqmd->bhgqm', qrg, k_g) * scale
        o_slc = jnp.einsum('bhgqm,bhqmd->bqhgd', masked_softmax(s, m_slc), v_g)
        kw, vw, kwpos = slab_fn(bi)
        dpos = qpos[:, None] - kwpos[None, :]
        m_win = (dpos >= 0) & (dpos <= WINDOW) & (kwpos[None, :] >= 0)
        s = jnp.einsum('bqhgd,bkhd->bhgqk', qrg, kw.astype(f32)) * scale
        o_swa = jnp.einsum('bhgqk,bkhd->bqhgd', masked_softmax(s, m_win), vw.astype(f32))
        gt = gt.reshape(B, qb, NSA_HKV, NSA_G, 3)
        o = gt[..., 0:1] * o_cmp + gt[..., 1:2] * o_slc + gt[..., 2:3] * o_swa
        return o.reshape(B, qb, NSA_HQ, NSA_DH)

    def split_blocks(a):
        return jnp.moveaxis(a.reshape((B, nblk, qb) + a.shape[2:]), 1, 0)

    o = lax.map(one_block, (jnp.arange(nblk), split_blocks(q), split_blocks(q_rot), split_blocks(gate)))
    return jnp.moveaxis(o, 0, 1).reshape(B, L, NSA_HQ, NSA_DH)


def nsa_mixer(x, pos0, past, w_in, q_norm, k_norm, pe_k, pe_v, w_ck, w_cv, w_out):
    B, L, _ = x.shape
    q, q_rot, gate, kc, vc, ks, vs, kw, vw = nsa_project(x, pos0 + jnp.arange(L), w_in, q_norm, k_norm)
    if past is None:
        kc_all, vc_all, ks_all, vs_all = kc, vc, ks, vs
        qb = min(Q_BLOCK, L)
        padw = ((0, 0), (WINDOW, 0), (0, 0), (0, 0))
        kw_pad, vw_pad = jnp.pad(kw, padw), jnp.pad(vw, padw)

        def slab_fn(bi):
            start = bi * qb
            size = WINDOW + qb
            return (lax.dynamic_slice_in_dim(kw_pad, start, size, 1),
                    lax.dynamic_slice_in_dim(vw_pad, start, size, 1),
                    pos0 + start - WINDOW + jnp.arange(size))
        n_keep = min(WINDOW, L)
        win_k, win_v = kw[:, L - n_keep:], vw[:, L - n_keep:]
    else:
        pc_k, pc_v, ps_k, ps_v, buf_k, buf_v = past
        cat = lambda a, b: jnp.concatenate([a.astype(b.dtype), b], axis=1)
        kc_all, vc_all, ks_all, vs_all = cat(pc_k, kc), cat(pc_v, vc), cat(ps_k, ks), cat(ps_v, vs)
        kw_all, vw_all = cat(buf_k, kw), cat(buf_v, vw)
        wb = buf_k.shape[1]
        kw_pos = pos0 - wb + jnp.arange(wb + L)
        slab_fn = lambda bi: (kw_all, vw_all, kw_pos)
        win_k, win_v = kw_all[:, L:], vw_all[:, L:]
    o = nsa_attend(q, q_rot, gate, pos0, kc_all, vc_all, ks_all, vs_all, slab_fn, k_norm[0], pe_k, pe_v, w_ck, w_cv)
    y = o.reshape(B, L, NSA_HQ * NSA_DH).astype(x.dtype) @ w_out
    return y, (kc, vc, ks, vs, win_k, win_v)


def paged_rows(pool, page_table):
    g = pool[page_table]
    return g.reshape((g.shape[0], g.shape[1] * g.shape[2]) + g.shape[3:])


def setup_inputs(seed: int = 0) -> dict:
    key = jax.random.key(seed)
    keys = iter(jax.random.split(key, 48))
    f32 = jnp.float32
    D = D_MODEL

    def nrm(shape, scale=1.0):
        return jax.random.normal(next(keys), shape, f32) * scale

    def gain(shape):
        return 1.0 + nrm(shape, 0.05)

    n_pages = PAST_LEN // PAGE_SIZE
    n_used = DEC_BATCH * n_pages
    n_pool = n_used + n_used // 4
    wb = min(WINDOW, PAST_LEN)
    pool_shape = (N_C, n_pool, PAGE_SIZE, NSA_HKV, NSA_DH)
    x_prompt = nrm((BATCH, SEQ, D))
    x_sample = nrm((DEC_BATCH, DEC_SEQ, D))
    state_delta = nrm((N_A, DEC_BATCH, DN_HV, DN_DK, DN_DV), 0.1)
    state_conv = nrm((N_A, DEC_BATCH, DN_CONV - 1, DN_CONV_DIM))
    cache_swa_k = nrm((N_C, DEC_BATCH, wb, NSA_HKV, NSA_DH))
    cache_swa_v = nrm((N_C, DEC_BATCH, wb, NSA_HKV, NSA_DH))
    cache_cmp_k = nrm(pool_shape)
    cache_cmp_v = nrm(pool_shape)
    cache_slc_k = nrm(pool_shape)
    cache_slc_v = nrm(pool_shape)
    page_table = jax.random.permutation(next(keys), n_pool)[:n_used].reshape(DEC_BATCH, n_pages).astype(jnp.int32)
    norm_mix = gain((DEPTH, D))
    norm_ffn = gain((DEPTH, D))
    dn_w_in = nrm((N_A, D, DN_IN), D ** -0.5)
    dn_conv_w = nrm((N_A, DN_CONV, DN_CONV_DIM), DN_CONV ** -0.5)
    dn_a_log = jnp.log(jax.random.uniform(next(keys), (N_A, DN_HV), f32, 1.0, 16.0))
    dt = jnp.exp(jax.random.uniform(next(keys), (N_A, DN_HV), f32, math.log(1e-3), math.log(1e-1)))
    dn_dt_bias = dt + jnp.log(-jnp.expm1(-dt))
    dn_norm = gain((N_A, DN_DV))
    dn_w_out = nrm((N_A, DN_VW, D), DN_VW ** -0.5)
    sg_w_in = nrm((N_B, D, 2 * SG_WIDTH), D ** -0.5)
    sg_ln_g = gain((N_B, SG_WIDTH))
    sg_ln_b = nrm((N_B, SG_WIDTH), 0.02)
    sg_w_spatial = nrm((N_B, SG_GROUPS, SG_CHUNK, SG_CHUNK), SG_CHUNK ** -0.5)
    sg_b_spatial = 1.0 + nrm((N_B, SG_GROUPS, SG_CHUNK), 0.1)
    sg_w_out = nrm((N_B, SG_WIDTH, D), SG_WIDTH ** -0.5)
    nsa_w_in = nrm((N_C, D, NSA_IN), D ** -0.5)
    nsa_q_norm = gain((N_C, NSA_DH))
    nsa_k_norm = gain((N_C, 3, NSA_DH))
    nsa_cmp_pe_k = nrm((N_C, CMP_BLOCK, NSA_DH), 0.1)
    nsa_cmp_pe_v = nrm((N_C, CMP_BLOCK, NSA_DH), 0.1)
    nsa_cmp_w_k = nrm((N_C, CMP_BLOCK * NSA_DH, NSA_DH), (CMP_BLOCK * NSA_DH) ** -0.5)
    nsa_cmp_w_v = nrm((N_C, CMP_BLOCK * NSA_DH, NSA_DH), (CMP_BLOCK * NSA_DH) ** -0.5)
    nsa_w_out = nrm((N_C, NSA_HQ * NSA_DH, D), (NSA_HQ * NSA_DH) ** -0.5)
    ffn_w_gate = nrm((DEPTH, D, FFN_HIDDEN), D ** -0.5)
    ffn_w_up = nrm((DEPTH, D, FFN_HIDDEN), D ** -0.5)
    ffn_w_down = nrm((DEPTH, FFN_HIDDEN, D), FFN_HIDDEN ** -0.5)
    return {'x_prompt': x_prompt, 'x_sample': x_sample, 'state_delta': state_delta, 'state_conv': state_conv,
            'cache_swa_k': cache_swa_k, 'cache_swa_v': cache_swa_v, 'cache_cmp_k': cache_cmp_k,
            'cache_cmp_v': cache_cmp_v, 'cache_slc_k': cache_slc_k, 'cache_slc_v': cache_slc_v,
            'page_table': page_table, 'norm_mix': norm_mix, 'norm_ffn': norm_ffn,
            'dn_w_in': dn_w_in, 'dn_conv_w': dn_conv_w, 'dn_a_log': dn_a_log, 'dn_dt_bias': dn_dt_bias,
            'dn_norm': dn_norm, 'dn_w_out': dn_w_out, 'sg_w_in': sg_w_in, 'sg_ln_g': sg_ln_g,
            'sg_ln_b': sg_ln_b, 'sg_w_spatial': sg_w_spatial, 'sg_b_spatial': sg_b_spatial,
            'sg_w_out': sg_w_out, 'nsa_w_in': nsa_w_in, 'nsa_q_norm': nsa_q_norm, 'nsa_k_norm': nsa_k_norm,
            'nsa_cmp_pe_k': nsa_cmp_pe_k, 'nsa_cmp_pe_v': nsa_cmp_pe_v, 'nsa_cmp_w_k': nsa_cmp_w_k,
            'nsa_cmp_w_v': nsa_cmp_w_v, 'nsa_w_out': nsa_w_out, 'ffn_w_gate': ffn_w_gate,
            'ffn_w_up': ffn_w_up, 'ffn_w_down': ffn_w_down}


def reference(x_prompt, x_sample, state_delta, state_conv, cache_swa_k, cache_swa_v, cache_cmp_k, cache_cmp_v,
              cache_slc_k, cache_slc_v, page_table, norm_mix, norm_ffn, dn_w_in, dn_conv_w, dn_a_log, dn_dt_bias,
              dn_norm, dn_w_out, sg_w_in, sg_ln_g, sg_ln_b, sg_w_spatial, sg_b_spatial, sg_w_out, nsa_w_in,
              nsa_q_norm, nsa_k_norm, nsa_cmp_pe_k, nsa_cmp_pe_v, nsa_cmp_w_k, nsa_cmp_w_v, nsa_w_out,
              ffn_w_gate, ffn_w_up, ffn_w_down):
    past_len = page_table.shape[1] * cache_cmp_k.shape[2]
    bp = x_prompt.shape[0]
    hp, hs = x_prompt, x_sample
    dn_S_p, dn_S_s, dn_c_p, dn_c_s = [], [], [], []
    sg_v_s = []
    c_p = ([], [], [], [], [], [])
    c_s = ([], [], [], [], [], [])
    for i in range(DEPTH):
        kind, j = i % N_MIXERS, i // N_MIXERS
        ap, a_s = rms_norm(hp, norm_mix[i]), rms_norm(hs, norm_mix[i])
        if kind == 0:
            prm = (dn_w_in[j], dn_conv_w[j], dn_a_log[j], dn_dt_bias[j], dn_norm[j], dn_w_out[j])
            zero_buf = jnp.zeros((bp, DN_CONV - 1, DN_CONV_DIM), hp.dtype)
            zero_S = jnp.zeros((bp, DN_HV, DN_DK, DN_DV), state_delta.dtype)
            mp, Sp, cp = delta_mixer(ap, zero_buf, zero_S, *prm)
            ms, Ss, cs = delta_mixer(a_s, state_conv[j], state_delta[j], *prm)
            dn_S_p.append(Sp); dn_S_s.append(Ss); dn_c_p.append(cp); dn_c_s.append(cs)
        elif kind == 1:
            prm = (sg_w_in[j], sg_ln_g[j], sg_ln_b[j], sg_w_spatial[j], sg_b_spatial[j], sg_w_out[j])
            mp, _ = spatial_gating_mixer(ap, *prm)
            ms, vs = spatial_gating_mixer(a_s, *prm)
            sg_v_s.append(vs)
        else:
            prm = (nsa_w_in[j], nsa_q_norm[j], nsa_k_norm[j], nsa_cmp_pe_k[j], nsa_cmp_pe_v[j],
                   nsa_cmp_w_k[j], nsa_cmp_w_v[j], nsa_w_out[j])
            past = (paged_rows(cache_cmp_k[j], page_table), paged_rows(cache_cmp_v[j], page_table),
                    paged_rows(cache_slc_k[j], page_table), paged_rows(cache_slc_v[j], page_table),
                    cache_swa_k[j], cache_swa_v[j])
            mp, new_p = nsa_mixer(ap, 0, None, *prm)
            ms, new_s = nsa_mixer(a_s, past_len, past, *prm)
            for lst, a in zip(c_p, new_p):
                lst.append(a)
            for lst, a in zip(c_s, new_s):
                lst.append(a)
        hp = hp + mp.astype(hp.dtype)
        hs = hs + ms.astype(hs.dtype)
        hp = hp + swiglu(rms_norm(hp, norm_ffn[i]), ffn_w_gate[i], ffn_w_up[i], ffn_w_down[i]).astype(hp.dtype)
        hs = hs + swiglu(rms_norm(hs, norm_ffn[i]), ffn_w_gate[i], ffn_w_up[i], ffn_w_down[i]).astype(hs.dtype)
    return (hp, hs,
            jnp.stack(dn_S_p), jnp.stack(dn_S_s), jnp.stack(dn_c_p), jnp.stack(dn_c_s),
            jnp.stack(sg_v_s),
            jnp.stack(c_p[4]), jnp.stack(c_p[5]), jnp.stack(c_s[4]), jnp.stack(c_s[5]),
            jnp.stack(c_p[0]), jnp.stack(c_p[1]), jnp.stack(c_p[2]), jnp.stack(c_p[3]),
            jnp.stack(c_s[0]), jnp.stack(c_s[1]), jnp.stack(c_s[2]), jnp.stack(c_s[3]))
```

```python
import functools
import math

import numpy as np
import jax
import jax.numpy as jnp
from jax import lax
from jax.experimental import pallas as pl
from jax.experimental.pallas import tpu as pltpu

F32 = jnp.float32
BF16 = jnp.bfloat16
HIGHEST = lax.Precision.HIGHEST

N_MIXERS = 3
NORM_EPS = 1e-6
LN_EPS = 1e-5
L2_EPS = 1e-6
DN_DK = 128
DN_DV = 128
DN_HK = 8
DN_HV = 16
DN_CONV = 4
DN_CHUNK = 64
DN_QK = DN_HK * DN_DK
DN_VW = DN_HV * DN_DV
DN_CONV_DIM = 2 * DN_QK + DN_VW
DN_HEADS_ALL = DN_CONV_DIM // DN_DK
SG_CHUNK = 128
SG_GROUPS = 8
NSA_DH = 64
NSA_HQ = 16
NSA_HKV = 4
NSA_G = NSA_HQ // NSA_HKV
CMP_STRIDE = 16
CMP_BLOCK = 32
SLC_BLOCK = 64
N_SEL = 16
WINDOW = 512
Q_BLOCK = 128
FORCE_BONUS = 1e4
ROPE_THETA = 10000.0
NSA_QW = NSA_HQ * NSA_DH
NSA_KW = NSA_HKV * NSA_DH
SEG_W = CMP_STRIDE * NSA_KW
NEG = -1e30

LANE = 128
SUBLANE = 8
VMEM_LIMIT = 56 * 1024 * 1024


def _cparams(sem):
    return pltpu.CompilerParams(dimension_semantics=sem, vmem_limit_bytes=VMEM_LIMIT)


def _round_up(n, m):
    return -(-n // m) * m


def _pick(n, cands):
    for c in cands:
        if n % c == 0:
            return c
    raise ValueError(f"no tile for {n} in {cands}")


def _dot(a, b):
    return jnp.dot(a, b, preferred_element_type=F32)


def _dot_nt(a, b, precision=None):
    return lax.dot_general(a, b, (((1,), (1,)), ((), ())), precision=precision, preferred_element_type=F32)


def _dotf(a, b):
    return jnp.dot(a, b, precision=HIGHEST, preferred_element_type=F32)


def _rms(x, g, eps=NORM_EPS):
    return x * lax.rsqrt(jnp.mean(x * x, -1, keepdims=True) + eps) * g


def _resident(shape):
    nd = len(shape)
    return pl.BlockSpec(shape, lambda *_: (0,) * nd, pipeline_mode=pl.Buffered(1))


def _rms_matmul_body(x_ref, g_ref, w_ref, o_ref, *, cn, act):
    xb = _rms(x_ref[...], g_ref[...]).astype(BF16)
    for c0 in range(0, o_ref.shape[-1], cn):
        y = _dot(xb, w_ref[:, c0:c0 + cn])
        if act == "gelu":
            y = jax.nn.gelu(y)
        o_ref[:, c0:c0 + cn] = y


def rms_matmul(x, g, w, *, act=None):
    T, D = x.shape
    N = w.shape[1]
    tm = _pick(T, (256, 128, 64, 32, 16, 8))
    k = 1
    while not (N % k == 0 and (N // k) % LANE == 0 and N // k <= 1024):
        k += 1
    return pl.pallas_call(
        functools.partial(_rms_matmul_body, cn=N // k, act=act),
        out_shape=jax.ShapeDtypeStruct((T, N), F32),
        grid=(T // tm,),
        in_specs=[pl.BlockSpec((tm, D), lambda i: (i, 0)), _resident((1, D)), _resident((D, N))],
        out_specs=pl.BlockSpec((tm, N), lambda i: (i, 0)),
        compiler_params=_cparams(("parallel",)),
        name="rms_matmul",
    )(x, g.reshape(1, D), w)


def _ffn_body(x_ref, g_ref, wg_ref, wu_ref, wd_ref, o_ref, *, hc):
    x = x_ref[...]
    xb = _rms(x, g_ref[...]).astype(BF16)
    acc = x
    for c0 in range(0, wg_ref.shape[1], hc):
        gate = _dot(xb, wg_ref[:, c0:c0 + hc])
        up = _dot(xb, wu_ref[:, c0:c0 + hc])
        a = (jax.nn.silu(gate) * up).astype(BF16)
        acc = acc + _dot(a, wd_ref[c0:c0 + hc, :])
    o_ref[...] = acc


def ffn(x, g, wg, wu, wd):
    T, D = x.shape
    H = wg.shape[1]
    tm = _pick(T, (512, 256, 128, 64, 32, 16, 8))
    hc = H // 2 if (H // 2) % LANE == 0 else H
    return pl.pallas_call(
        functools.partial(_ffn_body, hc=hc),
        out_shape=jax.ShapeDtypeStruct((T, D), F32),
        grid=(T // tm,),
        in_specs=[pl.BlockSpec((tm, D), lambda i: (i, 0)), _resident((1, D)),
                  _resident((D, H)), _resident((D, H)), _resident((H, D))],
        out_specs=pl.BlockSpec((tm, D), lambda i: (i, 0)),
        compiler_params=_cparams(("parallel",)),
        name="ffn",
    )(x, g.reshape(1, D), wg, wu, wd)


def _mm_res_body(a_ref, w_ref, r_ref, o_ref):
    o_ref[...] = r_ref[...] + _dot(a_ref[...].astype(BF16), w_ref[...])


def mm_res(a, w, res):
    T, K = a.shape
    D = w.shape[1]
    tm = _pick(T, (512, 256, 128, 64, 32, 16, 8))
    return pl.pallas_call(
        _mm_res_body,
        out_shape=jax.ShapeDtypeStruct((T, D), F32),
        grid=(T // tm,),
        in_specs=[pl.BlockSpec((tm, K), lambda i: (i, 0)), _resident((K, D)),
                  pl.BlockSpec((tm, D), lambda i: (i, 0))],
        out_specs=pl.BlockSpec((tm, D), lambda i: (i, 0)),
        compiler_params=_cparams(("parallel",)),
        name="mm_res",
    )(a, w, res)


def _dn_gates_body(p_ref, alog_ref, dtb_ref, beta_ref, g_ref):
    p = p_ref[...]
    beta_ref[...] = jax.nn.sigmoid(p[:, :DN_HV])
    g_ref[...] = -jnp.exp(alog_ref[...]) * jax.nn.softplus(p[:, DN_HV:2 * DN_HV] + dtb_ref[...])


def dn_gates(proj, a_log, dt_bias):
    T = proj.shape[0]
    tm = _pick(T, (512, 256, 128, 64, 32, 16, 8))
    col_blk = (DN_CONV_DIM + DN_VW) // LANE
    return pl.pallas_call(
        _dn_gates_body,
        out_shape=(jax.ShapeDtypeStruct((T, DN_HV), F32),) * 2,
        grid=(T // tm,),
        in_specs=[pl.BlockSpec((tm, LANE), lambda i: (i, col_blk)), _resident((1, DN_HV)), _resident((1, DN_HV))],
        out_specs=(pl.BlockSpec((tm, DN_HV), lambda i: (i, 0)),) * 2,
        compiler_params=_cparams(("parallel",)),
        name="dn_gates",
    )(proj, a_log.reshape(1, DN_HV), dt_bias.reshape(1, DN_HV))


def _dn_conv_body(x_ref, halo_ref, w_ref, o_ref):
    for h in range(DN_HEADS_ALL):
        sl = slice(h * DN_DK, (h + 1) * DN_DK)
        x = x_ref[:, sl]
        xc = jnp.concatenate([halo_ref[0, :, sl], x], axis=0)
        w = w_ref[:, sl]
        acc = x * w[DN_CONV - 1:DN_CONV]
        for k in range(1, DN_CONV):
            shifted = pltpu.roll(xc, k, 0)[SUBLANE:]
            acc = acc + shifted * w[DN_CONV - 1 - k:DN_CONV - k]
        c = jax.nn.silu(acc)
        if h < 2 * DN_HK:
            c = c * lax.rsqrt(jnp.sum(c * c, -1, keepdims=True) + L2_EPS)
        if h < DN_HK:
            c = c * DN_DK ** -0.5
        o_ref[h] = c


def dn_conv(x, halo, conv_w, rows, tt):
    return pl.pallas_call(
        _dn_conv_body,
        out_shape=jax.ShapeDtypeStruct((DN_HEADS_ALL, rows, DN_DK), F32),
        grid=(rows // tt,),
        in_specs=[pl.BlockSpec((tt, DN_CONV_DIM), lambda i: (i, 0)),
                  pl.BlockSpec((1, SUBLANE, DN_CONV_DIM), lambda i: (i, 0, 0)),
                  _resident((DN_CONV, DN_CONV_DIM))],
        out_specs=pl.BlockSpec((DN_HEADS_ALL, tt, DN_DK), lambda i: (0, i, 0)),
        compiler_params=_cparams(("parallel",)),
        name="dn_conv",
    )(x, halo, conv_w)


def _delta_body(q_ref, k_ref, v_ref, g_ref, b_ref, s0_ref, o_ref, sout_ref, S_ref, *, hb):
    c = pl.program_id(2)
    C = DN_CHUNK

    @pl.when(c == 0)
    def _():
        S_ref[...] = s0_ref[0]

    ri = lax.broadcasted_iota(jnp.int32, (C, C), 0)
    ci = lax.broadcasted_iota(jnp.int32, (C, C), 1)
    tri = ri >= ci
    stri = ri > ci
    eye = jnp.where(ri == ci, 1.0, 0.0).astype(F32)
    rep = DN_HV // DN_HK
    for j in range(hb):
        q = q_ref[j // rep]
        k = k_ref[j // rep]
        v = v_ref[j]
        grow = g_ref[j, 0]
        brow = b_ref[j, 0]
        gcol = jnp.sum(eye * grow, -1, keepdims=True)
        bcol = jnp.sum(eye * brow, -1, keepdims=True)
        gc_col = jnp.sum(jnp.where(tri, grow, 0.0), -1, keepdims=True)
        gc_row = jnp.sum(jnp.where(ri <= ci, gcol, 0.0), 0, keepdims=True)
        decay = jnp.exp(jnp.where(tri, gc_col - gc_row, NEG))
        kb = k * bcol
        a_mat = jnp.where(stri, _dot_nt(kb, k, HIGHEST) * decay, 0.0)
        eg = jnp.exp(gc_col)
        rhs = jnp.concatenate([v * bcol, kb * eg], axis=-1)
        p = -a_mat
        t = eye + p
        for _ in range(int(math.log2(C)) - 1):
            p = _dotf(p, p)
            t = t + _dotf(t, p)
        sol = _dotf(t, rhs)
        u = sol[:, :DN_DV]
        w = sol[:, DN_DV:]
        qk = jnp.where(tri, _dot_nt(q, k, HIGHEST) * decay, 0.0)
        qg = q * eg
        g_last = gc_col[C - 1:C]
        kdec = k * jnp.exp(g_last - gc_col)
        S = S_ref[j]
        v_new = u - _dotf(w, S)
        o_ref[j] = _dotf(qg, S) + _dotf(qk, v_new)
        S_ref[j] = S * jnp.exp(g_last) + lax.dot_general(
            kdec, v_new, (((0,), (0,)), ((), ())), precision=HIGHEST, preferred_element_type=F32)

    @pl.when(c == pl.num_programs(2) - 1)
    def _():
        sout_ref[0] = S_ref[...]


def delta_rule(qkv_hm, g_rows, b_rows, s0, B, n):
    hb = 4
    rep = DN_HV // DN_HK
    C = DN_CHUNK
    nq = hb // rep
    return pl.pallas_call(
        functools.partial(_delta_body, hb=hb),
        out_shape=(jax.ShapeDtypeStruct((DN_HV, B * n * C, DN_DV), F32),
                   jax.ShapeDtypeStruct((B, DN_HV, DN_DK, DN_DV), F32)),
        grid=(B, DN_HV // hb, n),
        in_specs=[
            pl.BlockSpec((nq, C, DN_DK), lambda b, h, c: (h, b * n + c, 0)),
            pl.BlockSpec((nq, C, DN_DK), lambda b, h, c: (DN_HK // nq + h, b * n + c, 0)),
            pl.BlockSpec((hb, C, DN_DV), lambda b, h, c: (2 * DN_HK // hb + h, b * n + c, 0)),
            pl.BlockSpec((hb, 1, 1, C), lambda b, h, c: (h, b * n + c, 0, 0)),
            pl.BlockSpec((hb, 1, 1, C), lambda b, h, c: (h, b * n + c, 0, 0)),
            pl.BlockSpec((1, hb, DN_DK, DN_DV), lambda b, h, c: (b, h, 0, 0)),
        ],
        out_specs=(pl.BlockSpec((hb, C, DN_DV), lambda b, h, c: (h, b * n + c, 0)),
                   pl.BlockSpec((1, hb, DN_DK, DN_DV), lambda b, h, c: (b, h, 0, 0))),
        scratch_shapes=[pltpu.VMEM((hb, DN_DK, DN_DV), F32)],
        compiler_params=_cparams(("parallel", "parallel", "arbitrary")),
        name="delta_rule",
    )(qkv_hm, qkv_hm, qkv_hm, g_rows, b_rows, s0)


def _dn_out_body(o_ref, z_ref, g_ref, w_ref, r_ref, out_ref):
    parts = []
    for h in range(DN_HV):
        z = z_ref[:, h * DN_DV:(h + 1) * DN_DV]
        parts.append((_rms(o_ref[h], g_ref[...]) * jax.nn.silu(z)).astype(BF16))
    out_ref[...] = r_ref[...] + _dot(jnp.concatenate(parts, axis=-1), w_ref[...])


def dn_out(o_hm, proj, norm_g, w_out, res):
    T, D = res.shape
    tm = _pick(T, (256, 128, 64, 32, 16, 8))
    return pl.pallas_call(
        _dn_out_body,
        out_shape=jax.ShapeDtypeStruct((T, D), F32),
        grid=(T // tm,),
        in_specs=[pl.BlockSpec((DN_HV, tm, DN_DV), lambda i: (0, i, 0)),
                  pl.BlockSpec((tm, DN_VW), lambda i: (i, DN_CONV_DIM // DN_VW)),
                  _resident((1, DN_DV)), _resident((DN_VW, D)),
                  pl.BlockSpec((tm, D), lambda i: (i, 0))],
        out_specs=pl.BlockSpec((tm, D), lambda i: (i, 0)),
        compiler_params=_cparams(("parallel",)),
        name="dn_out",
    )(o_hm, proj, norm_g.reshape(1, DN_DV), w_out, res)


def _delta_group(h, proj, B, L, conv_buf, s0, conv_w, a_log, dt_bias, norm_g, w_out):
    C = DN_CHUNK
    Lp = _round_up(L, C)
    n = Lp // C
    qkv = proj[:, :DN_CONV_DIM].reshape(B, L, DN_CONV_DIM)
    new_buf = jnp.concatenate([conv_buf, qkv[:, max(L - (DN_CONV - 1), 0):]], axis=1)[:, -(DN_CONV - 1):]
    tt = _pick(Lp, (256, 128, 64))
    nt = Lp // tt
    if Lp == L:
        x = proj
    else:
        x = jnp.pad(qkv, ((0, 0), (0, Lp - L), (0, 0))).reshape(B * Lp, DN_CONV_DIM)
    first = jnp.concatenate([jnp.zeros((B, 1, SUBLANE - (DN_CONV - 1), DN_CONV_DIM), F32), conv_buf[:, None]], axis=2)
    if nt > 1:
        tails = x[:, :DN_CONV_DIM].reshape(B, nt, tt, DN_CONV_DIM)[:, :nt - 1, tt - SUBLANE:]
        halo = jnp.concatenate([first, tails], axis=1)
    else:
        halo = first
    qkv_hm = dn_conv(x, halo.reshape(B * nt, SUBLANE, DN_CONV_DIM), conv_w, B * Lp, tt)
    beta, g = dn_gates(proj, a_log, dt_bias)

    def rows(a):
        a = jnp.pad(a.reshape(B, L, DN_HV), ((0, 0), (0, Lp - L), (0, 0)))
        return a.reshape(B, n, C, DN_HV).transpose(3, 0, 1, 2).reshape(DN_HV, B * n, 1, C)

    o_hm, S = delta_rule(qkv_hm, rows(g), rows(beta), s0, B, n)
    if Lp != L:
        o_hm = o_hm.reshape(DN_HV, B, Lp, DN_DV)[:, :, :L].reshape(DN_HV, B * L, DN_DV)
    return dn_out(o_hm, proj, norm_g, w_out, h), S, new_buf


def _sg_body(y_ref, lng_ref, lnb_ref, wsp_ref, bsp_ref, a_ref, *v_out, width):
    u = y_ref[:, :width]
    v = y_ref[:, width:]
    xc = v - jnp.mean(v, -1, keepdims=True)
    vn = xc * lax.rsqrt(jnp.mean(xc * xc, -1, keepdims=True) + LN_EPS) * lng_ref[...] + lnb_ref[...]
    if v_out:
        v_out[0][...] = vn
    n = wsp_ref.shape[1]
    gw = width // SG_GROUPS
    tril = lax.broadcasted_iota(jnp.int32, (n, n), 0) >= lax.broadcasted_iota(jnp.int32, (n, n), 1)
    for g in range(SG_GROUPS):
        sl = slice(g * gw, (g + 1) * gw)
        wt = jnp.where(tril, wsp_ref[g], 0.0).astype(BF16)
        mixed = _dot(wt, vn[:, sl].astype(BF16)) + bsp_ref[:, g:g + 1]
        a_ref[:, sl] = u[:, sl] * mixed


def sg_mix(y, ln_g, ln_b, w_sp, b_sp, want_v):
    T = y.shape[0]
    W = y.shape[1] // 2
    n = SG_CHUNK
    out_shape = [jax.ShapeDtypeStruct((T, W), F32)]
    out_specs = [pl.BlockSpec((n, W), lambda i: (i, 0))]
    if want_v:
        out_shape.append(jax.ShapeDtypeStruct((T, W), F32))
        out_specs.append(pl.BlockSpec((n, W), lambda i: (i, 0)))
    return pl.pallas_call(
        functools.partial(_sg_body, width=W),
        out_shape=tuple(out_shape),
        grid=(T // n,),
        in_specs=[pl.BlockSpec((n, 2 * W), lambda i: (i, 0)), _resident((1, W)), _resident((1, W)),
                  _resident((SG_GROUPS, n, n)), _resident((n, SG_GROUPS))],
        out_specs=tuple(out_specs),
        compiler_params=_cparams(("parallel",)),
        name="sg_mix",
    )(y, ln_g.reshape(1, W), ln_b.reshape(1, W), w_sp, b_sp.T)


def _norm_rope_body(x_ref, g_ref, cos_ref, sin_ref, *out_refs, want_norm):
    xn = _rms(x_ref[0], g_ref[...])
    half = NSA_DH // 2
    xr = jnp.concatenate([xn[:, half:], xn[:, :half]], axis=-1)
    rot = xn * cos_ref[...] + xr * sin_ref[...]
    if want_norm:
        out_refs[0][0] = xn
    out_refs[-1][0] = rot


def norm_rope(x, g, cos, sin, want_norm=False):
    R, L, dh = x.shape
    tl = _pick(L, (1024, 512, 256, 128, 64, 32, 16, 8, L))
    blk = pl.BlockSpec((1, tl, dh), lambda r, l: (r, l, 0))
    n_out = 2 if want_norm else 1
    outs = pl.pallas_call(
        functools.partial(_norm_rope_body, want_norm=want_norm),
        out_shape=(jax.ShapeDtypeStruct((R, L, dh), F32),) * n_out,
        grid=(R, L // tl),
        in_specs=[blk, _resident((1, dh)), pl.BlockSpec((tl, dh), lambda r, l: (l, 0)),
                  pl.BlockSpec((tl, dh), lambda r, l: (l, 0))],
        out_specs=(blk,) * n_out,
        compiler_params=_cparams(("parallel", "parallel")),
        name="norm_rope",
    )(x, g.reshape(1, dh), cos, sin)
    return outs if want_norm else outs[0]


def _cmp_proj_body(x_ref, w_ref, o_ref):
    o_ref[...] = _dot(x_ref[...].astype(BF16), w_ref[...])


def cmp_proj(x, wbig):
    rows = x.shape[0]
    tm = _pick(rows, (512, 256, 128, 64, 32, 16, 8))
    return pl.pallas_call(
        _cmp_proj_body,
        out_shape=jax.ShapeDtypeStruct((rows, 2 * NSA_KW), F32),
        grid=(rows // tm,),
        in_specs=[pl.BlockSpec((tm, SEG_W), lambda i: (i, 0)), _resident((SEG_W, 2 * NSA_KW))],
        out_specs=pl.BlockSpec((tm, 2 * NSA_KW), lambda i: (i, 0)),
        compiler_params=_cparams(("parallel",)),
        name="cmp_proj",
    )(x, wbig)


def _page_copy(pool_ref, buf_ref, sem_ref, page, slot, p):
    return pltpu.make_async_copy(pool_ref.at[page], buf_ref.at[slot, p], sem_ref.at[slot])


def _fetch_pages(pt_ref, pool_ref, buf_ref, sem_ref, b, chunk, slot, pch):
    for p in range(pch):
        _page_copy(pool_ref, buf_ref, sem_ref, pt_ref[b, chunk * pch + p], slot, p).start()


def _wait_pages(pool_ref, buf_ref, sem_ref, slot, pch):
    for p in range(pch):
        _page_copy(pool_ref, buf_ref, sem_ref, 0, slot, p).wait()


def _cmp_proj_paged_body(pt_ref, pool_ref, w_ref, o_ref, buf_ref, sem_ref, *, pch):
    b, ch = pl.program_id(0), pl.program_id(1)
    nb, nch = pl.num_programs(0), pl.num_programs(1)
    step = b * nch + ch
    slot = lax.rem(step, 2)

    @pl.when(step == 0)
    def _():
        _fetch_pages(pt_ref, pool_ref, buf_ref, sem_ref, b, ch, slot, pch)

    @pl.when(step + 1 < nb * nch)
    def _():
        nxt = step + 1
        _fetch_pages(pt_ref, pool_ref, buf_ref, sem_ref, nxt // nch, lax.rem(nxt, nch), 1 - slot, pch)

    _wait_pages(pool_ref, buf_ref, sem_ref, slot, pch)
    x = buf_ref[slot].reshape(pch * buf_ref.shape[2], SEG_W)
    o_ref[...] = _dot(x.astype(BF16), w_ref[...])


def cmp_proj_paged(page_table, pool, wbig):
    B, n_pages = page_table.shape
    segs = pool.shape[1]
    pch = _pick(n_pages, (32, 16, 8, 4, 2, 1))
    nch = n_pages // pch
    return pl.pallas_call(
        functools.partial(_cmp_proj_paged_body, pch=pch),
        out_shape=jax.ShapeDtypeStruct((B * n_pages * segs, 2 * NSA_KW), F32),
        grid_spec=pltpu.PrefetchScalarGridSpec(
            num_scalar_prefetch=1,
            grid=(B, nch),
            in_specs=[pl.BlockSpec(memory_space=pl.ANY),
                      pl.BlockSpec((SEG_W, 2 * NSA_KW), lambda b, c, pt: (0, 0))],
            out_specs=pl.BlockSpec((pch * segs, 2 * NSA_KW), lambda b, c, pt: (b * nch + c, 0)),
            scratch_shapes=[pltpu.VMEM((2, pch, segs, SEG_W), F32), pltpu.SemaphoreType.DMA((2,))],
        ),
        compiler_params=_cparams(("arbitrary", "arbitrary")),
        name="cmp_proj_paged",
    )(page_table, pool, wbig)


def _cmp_finish_body(a_ref, pe_ref, w_ref, g_ref, o_ref, *, do_norm):
    a = a_ref[0]
    n = a.shape[0]
    pe = _dot(pe_ref[...].astype(BF16), w_ref[...])
    pe_term = pe[0:1, :NSA_KW] + pe[1:2, NSA_KW:]
    blk = a[:, :NSA_KW] + pltpu.roll(a[:, NSA_KW:], n - 1, 0) + pe_term
    for h in range(NSA_HKV):
        bh = blk[:, h * NSA_DH:(h + 1) * NSA_DH]
        if do_norm:
            bh = _rms(bh, g_ref[...])
        o_ref[0, h] = bh


def cmp_finish(a, pe_rows, wbig, g, B, n, do_norm):
    return pl.pallas_call(
        functools.partial(_cmp_finish_body, do_norm=do_norm),
        out_shape=jax.ShapeDtypeStruct((B, NSA_HKV, n, NSA_DH), F32),
        grid=(B,),
        in_specs=[pl.BlockSpec((1, n, 2 * NSA_KW), lambda b: (b, 0, 0)), _resident((SUBLANE, SEG_W)),
                  _resident((SEG_W, 2 * NSA_KW)), _resident((1, NSA_DH))],
        out_specs=pl.BlockSpec((1, NSA_HKV, n, NSA_DH), lambda b: (b, 0, 0, 0)),
        compiler_params=_cparams(("parallel",)),
        name="cmp_finish",
    )(a.reshape(B, n, 2 * NSA_KW), pe_rows, wbig, g.reshape(1, NSA_DH))


def _masked_softmax(s, mask):
    sm = jnp.where(mask, s, NEG)
    m = jnp.max(sm, -1, keepdims=True)
    p = jnp.where(mask, jnp.exp(s - m), 0.0)
    d = jnp.sum(p, -1, keepdims=True)
    return p / jnp.where(d > 0, d, 1.0)


def _topk_mask(score, k):
    nl = score.shape[-1]
    lane = lax.broadcasted_iota(jnp.int32, score.shape, 1).astype(F32)
    sel = jnp.zeros(score.shape, F32)
    work = score
    for _ in range(k):
        m = jnp.max(work, -1, keepdims=True)
        idx = jnp.min(jnp.where(work == m, lane, float(nl)), -1, keepdims=True)
        hit = lane == idx
        sel = jnp.where(hit, 1.0, sel)
        work = jnp.where(hit, -jnp.inf, work)
    return sel


def _select_blocks(imp, qpos, n_lanes_valid):
    blk = lax.broadcasted_iota(jnp.int32, imp.shape, 1)
    cur = qpos // SLC_BLOCK
    eligible = (blk * SLC_BLOCK <= qpos) & (blk < n_lanes_valid)
    forced = (blk == 0) | (blk == cur) | (blk == cur - 1)
    score = jnp.where(eligible, imp + jnp.where(forced, FORCE_BONUS, 0.0), -jnp.inf)
    return _topk_mask(score, N_SEL)


def _cmp_select_body(q_ref, kc_ref, vc_ref, m_ref, o_ref, sel_ref, *, n_slc):
    i = pl.program_id(2)
    qb = q_ref.shape[2]
    rows = NSA_G * qb
    q = q_ref[0].reshape(rows, NSA_DH) * NSA_DH ** -0.5
    kc = kc_ref[0, 0]
    ncp = kc.shape[0]
    s = _dot_nt(q, kc, HIGHEST)
    qpos = i * qb + (lax.broadcasted_iota(jnp.int32, (rows, 1), 0) & (qb - 1))
    kend = lax.broadcasted_iota(jnp.int32, (1, ncp), 1) * CMP_STRIDE + (CMP_BLOCK - 1)
    p = _masked_softmax(s, kend <= qpos)
    o_ref[0] = _dot(p.astype(BF16), vc_ref[0, 0].astype(BF16)).reshape(NSA_G, qb, NSA_DH)
    p3 = p.reshape(NSA_G, qb, ncp)
    psum = p3[0]
    for g in range(1, NSA_G):
        psum = psum + p3[g]
    imp = _dotf(psum, m_ref[...])
    sel_ref[0, 0] = _select_blocks(imp, qpos[:qb], n_slc)


def cmp_select(qn, kcmp, vcmp, m_mat, n_slc):
    B, _, L, dh = qn.shape
    ncp = kcmp.shape[2]
    nb = m_mat.shape[1]
    qb = Q_BLOCK
    kv_spec = pl.BlockSpec((1, 1, ncp, dh), lambda b, h, i: (b, h, 0, 0))
    return pl.pallas_call(
        functools.partial(_cmp_select_body, n_slc=n_slc),
        out_shape=(jax.ShapeDtypeStruct((B, NSA_HQ, L, dh), F32), jax.ShapeDtypeStruct((B, NSA_HKV, L, nb), F32)),
        grid=(B, NSA_HKV, L // qb),
        in_specs=[pl.BlockSpec((1, NSA_G, qb, dh), lambda b, h, i: (b, h, i, 0)), kv_spec, kv_spec,
                  _resident((ncp, nb))],
        out_specs=(pl.BlockSpec((1, NSA_G, qb, dh), lambda b, h, i: (b, h, i, 0)),
                   pl.BlockSpec((1, 1, qb, nb), lambda b, h, i: (b, h, i, 0))),
        compiler_params=_cparams(("parallel", "parallel", "parallel")),
        name="cmp_select",
    )(qn, kcmp, vcmp, m_mat)


def _flash_tile(q3, k, v, mask, carry):
    m, l, acc = carry
    G, R, dh = q3.shape
    tk = k.shape[0]
    s = _dot_nt(q3.reshape(G * R, dh), k).reshape(G, R, tk)
    mask3 = mask[None]
    m_new = jnp.maximum(m, jnp.max(jnp.where(mask3, s, NEG), -1, keepdims=True))
    alpha = jnp.exp(m - m_new)
    p = jnp.where(mask3, jnp.exp(s - m_new), 0.0)
    l = alpha * l + jnp.sum(p, -1, keepdims=True)
    pv = _dot(p.reshape(G * R, tk).astype(BF16), v).reshape(G, R, dh)
    return m_new, l, alpha * acc + pv


def _flash_init(G, R, dh):
    return (jnp.full((G, R, 1), NEG, F32), jnp.zeros((G, R, 1), F32), jnp.zeros((G, R, dh), F32))


def _flash_done(carry):
    _, l, acc = carry
    return acc / jnp.where(l > 0, l, 1.0)


def _slc_swa_body(q_ref, ks_ref, vs_ref, kw_ref, vw_ref, sel_ref, oc_ref, gt_ref, o_ref, *, tk):
    i = pl.program_id(2)
    qb = q_ref.shape[2]
    nb = sel_ref.shape[3]
    q3 = (q_ref[0] * NSA_DH ** -0.5).astype(BF16)
    qpos = i * qb + lax.broadcasted_iota(jnp.int32, (qb, 1), 0)
    sel = sel_ref[0, 0].astype(BF16)
    blk_i = lax.broadcasted_iota(jnp.int32, (nb, tk), 0)
    key_blk = lax.broadcasted_iota(jnp.int32, (nb, tk), 1) // SLC_BLOCK
    key_i = lax.broadcasted_iota(jnp.int32, (1, tk), 1)

    def slc_step(j, carry):
        start = pl.multiple_of(j * tk, tk)
        k = ks_ref[0, 0, pl.ds(start, tk), :]
        v = vs_ref[0, 0, pl.ds(start, tk), :]
        expand = jnp.where(blk_i == key_blk + j * (tk // SLC_BLOCK), 1.0, 0.0).astype(BF16)
        selk = _dot(sel, expand)
        mask = jnp.where(key_i + j * tk <= qpos, selk, 0.0) > 0.5
        return _flash_tile(q3, k, v, mask, carry)

    n_tiles = (i * qb + qb + tk - 1) // tk
    o_slc = _flash_done(lax.fori_loop(0, n_tiles, slc_step, _flash_init(NSA_G, qb, NSA_DH)))

    carry = _flash_init(NSA_G, qb, NSA_DH)
    lane = lax.broadcasted_iota(jnp.int32, (1, qb), 1)
    for t in range(WINDOW // qb + 1):
        jt = i - WINDOW // qb + t
        start = pl.multiple_of(jnp.maximum(jt, 0) * qb, qb)
        kpos = jt * qb + lane
        d = qpos - kpos
        mask = (d >= 0) & (d <= WINDOW) & (kpos >= 0)
        carry = _flash_tile(q3, kw_ref[0, 0, pl.ds(start, qb), :], vw_ref[0, 0, pl.ds(start, qb), :], mask, carry)
    o_swa = _flash_done(carry)

    gt = jax.nn.sigmoid(gt_ref[0, 0])
    parts = []
    for g in range(NSA_G):
        parts.append(gt[:, 3 * g:3 * g + 1] * oc_ref[0, g] + gt[:, 3 * g + 1:3 * g + 2] * o_slc[g]
                     + gt[:, 3 * g + 2:3 * g + 3] * o_swa[g])
    o_ref[0] = jnp.concatenate(parts, axis=-1)


def slc_swa(qr, ks, vs, kw, vw, sel, o_cmp, gates):
    B, _, L, dh = qr.shape
    nb = sel.shape[3]
    qb = Q_BLOCK
    tk = _pick(L, (512, 256, 128))
    q_spec = pl.BlockSpec((1, NSA_G, qb, dh), lambda b, h, i: (b, h, i, 0))
    kv_spec = pl.BlockSpec((1, 1, L, dh), lambda b, h, i: (b, h, 0, 0))
    return pl.pallas_call(
        functools.partial(_slc_swa_body, tk=tk),
        out_shape=jax.ShapeDtypeStruct((B, L, NSA_QW), F32),
        grid=(B, NSA_HKV, L // qb),
        in_specs=[q_spec, kv_spec, kv_spec, kv_spec, kv_spec,
                  pl.BlockSpec((1, 1, qb, nb), lambda b, h, i: (b, h, i, 0)), q_spec,
                  pl.BlockSpec((1, 1, qb, 3 * NSA_G), lambda b, h, i: (b, h, i, 0))],
        out_specs=pl.BlockSpec((1, qb, NSA_G * dh), lambda b, h, i: (b, i, h)),
        compiler_params=_cparams(("parallel", "parallel", "parallel")),
        name="slc_swa",
    )(qr, ks, vs, kw, vw, sel, o_cmp, gates)


def _sample_attn_body(pt_ref, qn_ref, qr_ref, kc_ref, vc_ref, kpool_ref, vpool_ref, ksn_ref, vsn_ref,
                      bk_ref, bv_ref, kwn_ref, vwn_ref, gt_ref, m_ref, o_ref,
                      kbuf, vbuf, ksem, vsem, sel_s, m_s, l_s, acc_s, *, pch, L, pos0, n_slc):
    b, ch = pl.program_id(0), pl.program_id(1)
    nb, nch = pl.num_programs(0), pl.num_programs(1)
    step = b * nch + ch
    slot = lax.rem(step, 2)
    R = NSA_G * L
    page = kbuf.shape[2]
    tk = pch * page

    @pl.when(step == 0)
    def _():
        _fetch_pages(pt_ref, kpool_ref, kbuf, ksem, b, ch, slot, pch)
        _fetch_pages(pt_ref, vpool_ref, vbuf, vsem, b, ch, slot, pch)

    @pl.when(step + 1 < nb * nch)
    def _():
        nxt = step + 1
        _fetch_pages(pt_ref, kpool_ref, kbuf, ksem, nxt // nch, lax.rem(nxt, nch), 1 - slot, pch)
        _fetch_pages(pt_ref, vpool_ref, vbuf, vsem, nxt // nch, lax.rem(nxt, nch), 1 - slot, pch)

    row = lax.broadcasted_iota(jnp.int32, (R, 1), 0)
    tq = row & (L - 1)
    qpos = pos0 + tq
    same_t = jnp.where((lax.broadcasted_iota(jnp.int32, (R, R), 0) & (L - 1))
                       == (lax.broadcasted_iota(jnp.int32, (R, R), 1) & (L - 1)), 1.0, 0.0)
    scale = NSA_DH ** -0.5

    @pl.when(ch == 0)
    def _():
        for h in range(NSA_HKV):
            kc = kc_ref[0, h]
            ncp = kc.shape[0]
            s = _dot_nt(qn_ref[0, h] * scale, kc, HIGHEST)
            kend = lax.broadcasted_iota(jnp.int32, (1, ncp), 1) * CMP_STRIDE + (CMP_BLOCK - 1)
            p = _masked_softmax(s, kend <= qpos)
            acc_s[0, h] = _dot(p.astype(BF16), vc_ref[0, h].astype(BF16))
            imp = _dotf(_dotf(same_t, p), m_ref[...])
            sel_s[h] = _select_blocks(imp, qpos, n_slc)
            m_s[h] = jnp.full((R, 1), NEG, F32)
            l_s[h] = jnp.zeros((R, 1), F32)
            acc_s[1, h] = jnp.zeros((R, NSA_DH), F32)

    _wait_pages(kpool_ref, kbuf, ksem, slot, pch)
    _wait_pages(vpool_ref, vbuf, vsem, slot, pch)
    nbl = sel_s.shape[2]
    blk_i = lax.broadcasted_iota(jnp.int32, (nbl, tk), 0)
    key_blk = lax.broadcasted_iota(jnp.int32, (nbl, tk), 1) // SLC_BLOCK + ch * (tk // SLC_BLOCK)
    expand = jnp.where(blk_i == key_blk, 1.0, 0.0).astype(BF16)
    kall = kbuf[slot].reshape(tk, NSA_KW)
    vall = vbuf[slot].reshape(tk, NSA_KW)
    for h in range(NSA_HKV):
        sl = slice(h * NSA_DH, (h + 1) * NSA_DH)
        q3 = (qr_ref[0, h] * scale).astype(BF16)[None]
        mask = _dot(sel_s[h].astype(BF16), expand) > 0.5
        carry = (m_s[h][None], l_s[h][None], acc_s[1, h][None])
        m, l, acc = _flash_tile(q3, kall[:, sl].astype(BF16), vall[:, sl].astype(BF16), mask, carry)
        m_s[h], l_s[h], acc_s[1, h] = m[0], l[0], acc[0]

    @pl.when(ch == nch - 1)
    def _():
        npad = ksn_ref.shape[1]
        tn = lax.broadcasted_iota(jnp.int32, (1, npad), 1)
        new_ok = (tn <= tq) & (tn < L)
        wb = bk_ref.shape[1]
        wpos = pos0 - wb + lax.broadcasted_iota(jnp.int32, (1, wb), 1)
        dw = qpos - wpos
        win_ok = (dw >= 0) & (dw <= WINDOW)
        cur_blk = pos0 // SLC_BLOCK
        for h in range(NSA_HKV):
            sl = slice(h * NSA_DH, (h + 1) * NSA_DH)
            q3 = (qr_ref[0, h] * scale).astype(BF16)[None]
            sel_new = sel_s[h][:, cur_blk:cur_blk + 1] > 0.5
            carry = (m_s[h][None], l_s[h][None], acc_s[1, h][None])
            carry = _flash_tile(q3, ksn_ref[0][:, sl].astype(BF16), vsn_ref[0][:, sl].astype(BF16),
                                new_ok & sel_new, carry)
            o_slc = _flash_done(carry)[0]
            carry = _flash_init(1, R, NSA_DH)
            carry = _flash_tile(q3, bk_ref[0][:, sl].astype(BF16), bv_ref[0][:, sl].astype(BF16), win_ok, carry)
            carry = _flash_tile(q3, kwn_ref[0][:, sl].astype(BF16), vwn_ref[0][:, sl].astype(BF16), new_ok, carry)
            o_swa = _flash_done(carry)[0]
            gt = jax.nn.sigmoid(gt_ref[0, h])
            o_ref[0, h] = gt[:, 0:1] * acc_s[0, h] + gt[:, 1:2] * o_slc + gt[:, 2:3] * o_swa


def sample_attn(page_table, qn, qr, kcmp, vcmp, kpool, vpool, ks_new, vs_new, buf_k, buf_v, kw_new, vw_new,
                gates, m_mat, L, pos0, n_slc):
    B, n_pages = page_table.shape
    page = kpool.shape[1]
    R = NSA_G * L
    pch = _pick(n_pages, (16, 8, 4, 2, 1))
    nch = n_pages // pch
    nbl = m_mat.shape[1]
    ncp = kcmp.shape[2]
    assert L & (L - 1) == 0 and pos0 % SLC_BLOCK == 0 and pos0 // SLC_BLOCK < nbl
    per_b = lambda *blk: pl.BlockSpec((1,) + blk, lambda b, c, pt: (b,) + (0,) * len(blk))
    return pl.pallas_call(
        functools.partial(_sample_attn_body, pch=pch, L=L, pos0=pos0, n_slc=n_slc),
        out_shape=jax.ShapeDtypeStruct((B, NSA_HKV, R, NSA_DH), F32),
        grid_spec=pltpu.PrefetchScalarGridSpec(
            num_scalar_prefetch=1,
            grid=(B, nch),
            in_specs=[per_b(NSA_HKV, R, NSA_DH), per_b(NSA_HKV, R, NSA_DH),
                      per_b(NSA_HKV, ncp, NSA_DH), per_b(NSA_HKV, ncp, NSA_DH),
                      pl.BlockSpec(memory_space=pl.ANY), pl.BlockSpec(memory_space=pl.ANY),
                      per_b(SUBLANE, NSA_KW), per_b(SUBLANE, NSA_KW),
                      per_b(buf_k.shape[1], NSA_KW), per_b(buf_k.shape[1], NSA_KW),
                      per_b(SUBLANE, NSA_KW), per_b(SUBLANE, NSA_KW),
                      per_b(NSA_HKV, R, 3),
                      pl.BlockSpec((ncp, nbl), lambda b, c, pt: (0, 0))],
            out_specs=per_b(NSA_HKV, R, NSA_DH),
            scratch_shapes=[pltpu.VMEM((2, pch, page, NSA_KW), F32), pltpu.VMEM((2, pch, page, NSA_KW), F32),
                            pltpu.SemaphoreType.DMA((2,)), pltpu.SemaphoreType.DMA((2,)),
                            pltpu.VMEM((NSA_HKV, R, nbl), F32), pltpu.VMEM((NSA_HKV, R, 1), F32),
                            pltpu.VMEM((NSA_HKV, R, 1), F32), pltpu.VMEM((2, NSA_HKV, R, NSA_DH), F32)],
        ),
        compiler_params=_cparams(("arbitrary", "arbitrary")),
        name="sample_attn",
    )(page_table, qn, qr, kcmp, vcmp, kpool, vpool, ks_new, vs_new, buf_k, buf_v, kw_new, vw_new, gates, m_mat)


def _cmp_to_slc(n_cmp_rows, n_cols, n_slc):
    k = np.arange(n_cmp_rows)[:, None] * CMP_STRIDE
    j = np.arange(n_cols)[None, :] * SLC_BLOCK
    ov = np.minimum(k + CMP_BLOCK, j + SLC_BLOCK) - np.maximum(k, j)
    m = np.clip(ov, 0, None).astype(np.float32) / CMP_BLOCK
    m[:, n_slc:] = 0.0
    return jnp.asarray(m)


def _rope_tables(pos0, L):
    half = NSA_DH // 2
    inv = ROPE_THETA ** (-jnp.arange(half, dtype=F32) / half)
    ang = (pos0 + jnp.arange(L)).astype(F32)[:, None] * inv[None, :]
    cos, sin = jnp.cos(ang), jnp.sin(ang)
    return jnp.concatenate([cos, cos], -1), jnp.concatenate([-sin, sin], -1)


def _cmp_weights(w, pe):
    w = w.reshape(2, CMP_STRIDE, NSA_DH, NSA_DH)
    eye = jnp.eye(NSA_HKV, dtype=F32)
    wbig = jnp.einsum("alde,hg->lhdage", w, eye).reshape(SEG_W, 2 * NSA_KW).astype(BF16)
    pe = pe.reshape(2, CMP_STRIDE, 1, NSA_DH)
    pe_rows = jnp.broadcast_to(pe, (2, CMP_STRIDE, NSA_HKV, NSA_DH)).reshape(2, SEG_W)
    return wbig, jnp.pad(pe_rows, ((0, SUBLANE - 2), (0, 0)))


def _nsa_split(proj, B, L):
    edges = [NSA_QW + i * NSA_KW for i in range(7)]
    cols = [proj[:, a:b] for a, b in zip([0] + edges, edges + [edges[-1] + 3 * NSA_HQ])]
    q, kc, vc, ks, vs, kw, vw, gate = cols
    tok = lambda a: a.reshape(B, L, NSA_HKV, NSA_DH)
    return q.reshape(B, L, NSA_HQ, NSA_DH), tok(kc), tok(vc), tok(ks), tok(vs), tok(kw), tok(vw), gate


def _head_major(a):
    B, L, H, dh = a.shape
    return a.transpose(0, 2, 1, 3).reshape(B * H, L, dh)


def _nsa_project(proj, B, L, pos0, q_norm, k_norm):
    q, kc, vc, ks, vs, kw, vw, gate = _nsa_split(proj, B, L)
    cos, sin = _rope_tables(pos0, L)
    qn, qr = norm_rope(_head_major(q), q_norm, cos, sin, want_norm=True)
    ksr = norm_rope(_head_major(ks), k_norm[1], cos, sin)
    kwr = norm_rope(_head_major(kw), k_norm[2], cos, sin)
    hm4 = lambda a, H: a.reshape(B, H, L, NSA_DH)
    tok = lambda a: hm4(a, NSA_HKV).transpose(0, 2, 1, 3)
    return hm4(qn, NSA_HQ), hm4(qr, NSA_HQ), gate, kc, vc, tok(ksr), vs, tok(kwr), vw, hm4(ksr, NSA_HKV), hm4(kwr, NSA_HKV)


def _nsa_prompt(h, proj, B, L, q_norm, k_norm, pe_k, pe_v, w_ck, w_cv, w_out):
    qn, qr, gate, kc, vc, ks, vs, kw, vw, ks_hm, kw_hm = _nsa_project(proj, B, L, 0, q_norm, k_norm)
    n_seg = L // CMP_STRIDE
    n_slc = L // SLC_BLOCK
    assert L % Q_BLOCK == 0 and n_slc <= LANE and n_slc >= N_SEL
    wk, pek = _cmp_weights(w_ck, pe_k)
    wv, pev = _cmp_weights(w_cv, pe_v)
    kcmp = cmp_finish(cmp_proj(kc.reshape(B * n_seg, SEG_W), wk), pek, wk, k_norm[0], B, n_seg, True)
    vcmp = cmp_finish(cmp_proj(vc.reshape(B * n_seg, SEG_W), wv), pev, wv, k_norm[0], B, n_seg, False)
    o_cmp, sel = cmp_select(qn, kcmp, vcmp, _cmp_to_slc(n_seg, LANE, n_slc), n_slc)
    hm = lambda a: a.transpose(0, 2, 1, 3).astype(BF16)
    gates = gate.reshape(B, L, NSA_HKV, 3 * NSA_G).transpose(0, 2, 1, 3)
    o = slc_swa(qr, ks_hm.astype(BF16), hm(vs), kw_hm.astype(BF16), hm(vw), sel, o_cmp, gates)
    n_keep = min(WINDOW, L)
    return mm_res(o.reshape(B * L, NSA_QW), w_out, h), (kc, vc, ks, vs, kw[:, L - n_keep:], vw[:, L - n_keep:])


def _nsa_sample(h, proj, B, L, past, page_table, q_norm, k_norm, pe_k, pe_v, w_ck, w_cv, w_out):
    pc_k, pc_v, ps_k, ps_v, buf_k, buf_v = past
    n_pool, page = pc_k.shape[:2]
    n_pages = page_table.shape[1]
    pos0 = n_pages * page
    qn, qr, gate, kc, vc, ks, vs, kw, vw, _, _ = _nsa_project(proj, B, L, pos0, q_norm, k_norm)
    assert L <= CMP_STRIDE and page % CMP_STRIDE == 0 and pos0 % SLC_BLOCK == 0
    n_seg = pos0 // CMP_STRIDE
    n_slc = -(-(pos0 + L) // SLC_BLOCK)
    nbl = _round_up(n_slc, LANE)
    wk, pek = _cmp_weights(w_ck, pe_k)
    wv, pev = _cmp_weights(w_cv, pe_v)
    seg_pool = lambda p: p.reshape(n_pool, page // CMP_STRIDE, SEG_W)
    kcmp = cmp_finish(cmp_proj_paged(page_table, seg_pool(pc_k), wk), pek, wk, k_norm[0], B, n_seg, True)
    vcmp = cmp_finish(cmp_proj_paged(page_table, seg_pool(pc_v), wv), pev, wv, k_norm[0], B, n_seg, False)
    rows = lambda a: a.reshape(B, NSA_HKV, NSA_G * L, NSA_DH)
    flat_pool = lambda p: p.reshape(n_pool, page, NSA_KW)
    new8 = lambda a: jnp.pad(a.reshape(B, L, NSA_KW), ((0, 0), (0, SUBLANE - L), (0, 0)))
    wb = buf_k.shape[1]
    gates = gate.reshape(B, L, NSA_HKV, NSA_G, 3).transpose(0, 2, 3, 1, 4).reshape(B, NSA_HKV, NSA_G * L, 3)
    o = sample_attn(page_table, rows(qn), rows(qr), kcmp, vcmp, flat_pool(ps_k), flat_pool(ps_v),
                    new8(ks), new8(vs), buf_k.reshape(B, wb, NSA_KW), buf_v.reshape(B, wb, NSA_KW),
                    new8(kw), new8(vw), gates, _cmp_to_slc(n_seg, nbl, n_slc), L, pos0, n_slc)
    o = o.reshape(B, NSA_HKV, NSA_G, L, NSA_DH).transpose(0, 3, 1, 2, 4).reshape(B * L, NSA_QW)
    win_k = jnp.concatenate([buf_k, kw], axis=1)[:, L:]
    win_v = jnp.concatenate([buf_v, vw], axis=1)[:, L:]
    return mm_res(o, w_out, h), (kc, vc, ks, vs, win_k, win_v)


def _pad_cols(w, n):
    return jnp.pad(w, ((0, 0), (0, n - w.shape[1]))).astype(BF16)


def kernel(x_prompt, x_sample, state_delta, state_conv, cache_swa_k, cache_swa_v, cache_cmp_k, cache_cmp_v, cache_slc_k, cache_slc_v, page_table, norm_mix, norm_ffn, dn_w_in, dn_conv_w, dn_a_log, dn_dt_bias, dn_norm, dn_w_out, sg_w_in, sg_ln_g, sg_ln_b, sg_w_spatial, sg_b_spatial, sg_w_out, nsa_w_in, nsa_q_norm, nsa_k_norm, nsa_cmp_pe_k, nsa_cmp_pe_v, nsa_cmp_w_k, nsa_cmp_w_v, nsa_w_out, ffn_w_gate, ffn_w_up, ffn_w_down):
    Bp, Lp, D = x_prompt.shape
    Bs, Ls, _ = x_sample.shape
    depth = norm_mix.shape[0]
    hp = x_prompt.reshape(Bp * Lp, D)
    hs = x_sample.reshape(Bs * Ls, D)
    dn_S_p, dn_S_s, dn_c_p, dn_c_s, sg_v_s = [], [], [], [], []
    c_p = ([], [], [], [], [], [])
    c_s = ([], [], [], [], [], [])
    for i in range(depth):
        kind, j = i % N_MIXERS, i // N_MIXERS
        if kind == 0:
            w_in = _pad_cols(dn_w_in[j], _round_up(dn_w_in.shape[2], LANE))
            w_out = dn_w_out[j].astype(BF16)
            prm = (dn_conv_w[j], dn_a_log[j], dn_dt_bias[j], dn_norm[j], w_out)
            proj_p = rms_matmul(hp, norm_mix[i], w_in)
            proj_s = rms_matmul(hs, norm_mix[i], w_in)
            zero_buf = jnp.zeros((Bp, DN_CONV - 1, DN_CONV_DIM), F32)
            zero_S = jnp.zeros((Bp, DN_HV, DN_DK, DN_DV), F32)
            hp, Sp, cp = _delta_group(hp, proj_p, Bp, Lp, zero_buf, zero_S, *prm)
            hs, Ss, cs = _delta_group(hs, proj_s, Bs, Ls, state_conv[j], state_delta[j], *prm)
            dn_S_p.append(Sp); dn_S_s.append(Ss); dn_c_p.append(cp); dn_c_s.append(cs)
        elif kind == 1:
            W = sg_w_in.shape[2] // 2
            w_in = sg_w_in[j].astype(BF16)
            w_out = sg_w_out[j].astype(BF16)
            prm = (sg_ln_g[j], sg_ln_b[j], sg_w_spatial[j], sg_b_spatial[j])
            y_p = rms_matmul(hp, norm_mix[i], w_in, act="gelu")
            y_s = rms_matmul(hs, norm_mix[i], w_in, act="gelu")
            (a_p,) = sg_mix(y_p, *prm, want_v=False)
            y_s = jnp.pad(y_s.reshape(Bs, Ls, 2 * W), ((0, 0), (0, SG_CHUNK - Ls), (0, 0)))
            a_s, v_s = sg_mix(y_s.reshape(Bs * SG_CHUNK, 2 * W), *prm, want_v=True)
            unpad = lambda a: a.reshape(Bs, SG_CHUNK, W)[:, :Ls]
            hp = mm_res(a_p, w_out, hp)
            hs = mm_res(unpad(a_s).reshape(Bs * Ls, W), w_out, hs)
            sg_v_s.append(unpad(v_s))
        else:
            w_in = _pad_cols(nsa_w_in[j], _round_up(nsa_w_in.shape[2], LANE))
            w_out = nsa_w_out[j].astype(BF16)
            prm = (nsa_q_norm[j], nsa_k_norm[j], nsa_cmp_pe_k[j], nsa_cmp_pe_v[j], nsa_cmp_w_k[j], nsa_cmp_w_v[j], w_out)
            past = (cache_cmp_k[j], cache_cmp_v[j], cache_slc_k[j], cache_slc_v[j], cache_swa_k[j], cache_swa_v[j])
            proj_p = rms_matmul(hp, norm_mix[i], w_in)
            proj_s = rms_matmul(hs, norm_mix[i], w_in)
            hp, new_p = _nsa_prompt(hp, proj_p, Bp, Lp, *prm)
            hs, new_s = _nsa_sample(hs, proj_s, Bs, Ls, past, page_table, *prm)
            for lst, a in zip(c_p, new_p):
                lst.append(a)
            for lst, a in zip(c_s, new_s):
                lst.append(a)
        wg, wu, wd = ffn_w_gate[i].astype(BF16), ffn_w_up[i].astype(BF16), ffn_w_down[i].astype(BF16)
        hp = ffn(hp, norm_ffn[i], wg, wu, wd)
        hs = ffn(hs, norm_ffn[i], wg, wu, wd)
    return (hp.reshape(Bp, Lp, D), hs.reshape(Bs, Ls, D),
            jnp.stack(dn_S_p), jnp.stack(dn_S_s), jnp.stack(dn_c_p), jnp.stack(dn_c_s),
            jnp.stack(sg_v_s),
            jnp.stack(c_p[4]), jnp.stack(c_p[5]), jnp.stack(c_s[4]), jnp.stack(c_s[5]),
            jnp.stack(c_p[0]), jnp.stack(c_p[1]), jnp.stack(c_p[2]), jnp.stack(c_p[3]),
            jnp.stack(c_s[0]), jnp.stack(c_s[1]), jnp.stack(c_s[2]), jnp.stack(c_s[3]))
```

```python
import functools
import math

import numpy as np
import jax
import jax.numpy as jnp
from jax import lax
from jax.experimental import pallas as pl
from jax.experimental.pallas import tpu as pltpu

F32 = jnp.float32
BF16 = jnp.bfloat16
HIGHEST = lax.Precision.HIGHEST

N_MIXERS = 3
NORM_EPS = 1e-6
LN_EPS = 1e-5
L2_EPS = 1e-6
DN_DK = 128
DN_DV = 128
DN_HK = 8
DN_HV = 16
DN_CONV = 4
DN_CHUNK = 64
INV_BLOCK = 16
DN_QK = DN_HK * DN_DK
DN_VW = DN_HV * DN_DV
DN_CONV_DIM = 2 * DN_QK + DN_VW
DN_HEADS_ALL = DN_CONV_DIM // DN_DK
SG_CHUNK = 128
SG_GROUPS = 8
NSA_DH = 64
NSA_HQ = 16
NSA_HKV = 4
NSA_G = NSA_HQ // NSA_HKV
CMP_STRIDE = 16
CMP_BLOCK = 32
SLC_BLOCK = 64
N_SEL = 16
WINDOW = 512
Q_BLOCK = 128
FORCE_BONUS = 1e4
ROPE_THETA = 10000.0
NSA_QW = NSA_HQ * NSA_DH
NSA_KW = NSA_HKV * NSA_DH
SEG_W = CMP_STRIDE * NSA_KW
NEG = -1e30
QK_SCALE_LOG2 = NSA_DH ** -0.5 * math.log2(math.e)

LANE = 128
SUBLANE = 8
VMEM_LIMIT = 56 * 1024 * 1024


def _cparams(sem):
    return pltpu.CompilerParams(dimension_semantics=sem, vmem_limit_bytes=VMEM_LIMIT)


def _round_up(n, m):
    return -(-n // m) * m


def _pick(n, cands):
    for c in cands:
        if n % c == 0:
            return c
    raise ValueError(f"no tile for {n} in {cands}")


def _dot(a, b):
    return jnp.dot(a, b, preferred_element_type=F32)


def _dot_nt(a, b, precision=None):
    return lax.dot_general(a, b, (((1,), (1,)), ((), ())), precision=precision, preferred_element_type=F32)


def _dotf(a, b):
    return jnp.dot(a, b, precision=HIGHEST, preferred_element_type=F32)


def _split_bf16(a):
    hi = a.astype(BF16)
    return hi, (a - hi.astype(F32)).astype(BF16)


def _dot3_nt(a, b):
    ah, al = _split_bf16(a)
    bh, bl = _split_bf16(b)
    return _dot_nt(ah, bh) + (_dot_nt(ah, bl) + _dot_nt(al, bh))


def _rms(x, g, eps=NORM_EPS):
    return x * lax.rsqrt(jnp.mean(x * x, -1, keepdims=True) + eps) * g


def _resident(shape):
    nd = len(shape)
    return pl.BlockSpec(shape, lambda *_: (0,) * nd, pipeline_mode=pl.Buffered(1))


def _rms_matmul_body(x_ref, g_ref, w_ref, o_ref, *, cn, act):
    xb = _rms(x_ref[...], g_ref[...]).astype(BF16)
    for c0 in range(0, o_ref.shape[-1], cn):
        y = _dot(xb, w_ref[:, c0:c0 + cn])
        if act == "gelu":
            y = jax.nn.gelu(y)
        o_ref[:, c0:c0 + cn] = y


def rms_matmul(x, g, w, *, act=None):
    T, D = x.shape
    N = w.shape[1]
    tm = _pick(T, (256, 128, 64, 32, 16, 8))
    k = 1
    while not (N % k == 0 and (N // k) % LANE == 0 and N // k <= 1024):
        k += 1
    return pl.pallas_call(
        functools.partial(_rms_matmul_body, cn=N // k, act=act),
        out_shape=jax.ShapeDtypeStruct((T, N), F32),
        grid=(T // tm,),
        in_specs=[pl.BlockSpec((tm, D), lambda i: (i, 0)), _resident((1, D)), _resident((D, N))],
        out_specs=pl.BlockSpec((tm, N), lambda i: (i, 0)),
        compiler_params=_cparams(("parallel",)),
        name="rms_matmul",
    )(x, g.reshape(1, D), w)


def _ffn_body(x_ref, g_ref, wg_ref, wu_ref, wd_ref, o_ref, *, hc):
    x = x_ref[...]
    xb = _rms(x, g_ref[...]).astype(BF16)
    acc = x
    for c0 in range(0, wg_ref.shape[1], hc):
        gate = _dot(xb, wg_ref[:, c0:c0 + hc])
        up = _dot(xb, wu_ref[:, c0:c0 + hc])
        a = (jax.nn.silu(gate) * up).astype(BF16)
        acc = acc + _dot(a, wd_ref[c0:c0 + hc, :])
    o_ref[...] = acc


def ffn(x, g, wg, wu, wd):
    T, D = x.shape
    H = wg.shape[1]
    tm = _pick(T, (512, 256, 128, 64, 32, 16, 8))
    hc = H // 2 if (H // 2) % LANE == 0 else H
    return pl.pallas_call(
        functools.partial(_ffn_body, hc=hc),
        out_shape=jax.ShapeDtypeStruct((T, D), F32),
        grid=(T // tm,),
        in_specs=[pl.BlockSpec((tm, D), lambda i: (i, 0)), _resident((1, D)),
                  _resident((D, H)), _resident((D, H)), _resident((H, D))],
        out_specs=pl.BlockSpec((tm, D), lambda i: (i, 0)),
        compiler_params=_cparams(("parallel",)),
        name="ffn",
    )(x, g.reshape(1, D), wg, wu, wd)


def _mm_res_body(a_ref, w_ref, r_ref, o_ref):
    o_ref[...] = r_ref[...] + _dot(a_ref[...].astype(BF16), w_ref[...])


def mm_res(a, w, res):
    T, K = a.shape
    D = w.shape[1]
    tm = _pick(T, (512, 256, 128, 64, 32, 16, 8))
    return pl.pallas_call(
        _mm_res_body,
        out_shape=jax.ShapeDtypeStruct((T, D), F32),
        grid=(T // tm,),
        in_specs=[pl.BlockSpec((tm, K), lambda i: (i, 0)), _resident((K, D)),
                  pl.BlockSpec((tm, D), lambda i: (i, 0))],
        out_specs=pl.BlockSpec((tm, D), lambda i: (i, 0)),
        compiler_params=_cparams(("parallel",)),
        name="mm_res",
    )(a, w, res)


def _dn_gates_body(p_ref, alog_ref, dtb_ref, beta_ref, g_ref):
    p = p_ref[...]
    beta_ref[...] = jax.nn.sigmoid(p[:, :DN_HV])
    g_ref[...] = -jnp.exp(alog_ref[...]) * jax.nn.softplus(p[:, DN_HV:2 * DN_HV] + dtb_ref[...])


def dn_gates(proj, a_log, dt_bias):
    T = proj.shape[0]
    tm = _pick(T, (512, 256, 128, 64, 32, 16, 8))
    col_blk = (DN_CONV_DIM + DN_VW) // LANE
    return pl.pallas_call(
        _dn_gates_body,
        out_shape=(jax.ShapeDtypeStruct((T, DN_HV), F32),) * 2,
        grid=(T // tm,),
        in_specs=[pl.BlockSpec((tm, LANE), lambda i: (i, col_blk)), _resident((1, DN_HV)), _resident((1, DN_HV))],
        out_specs=(pl.BlockSpec((tm, DN_HV), lambda i: (i, 0)),) * 2,
        compiler_params=_cparams(("parallel",)),
        name="dn_gates",
    )(proj, a_log.reshape(1, DN_HV), dt_bias.reshape(1, DN_HV))


def _dn_conv_body(x_ref, halo_ref, w_ref, o_ref):
    for h in range(DN_HEADS_ALL):
        sl = slice(h * DN_DK, (h + 1) * DN_DK)
        x = x_ref[:, sl]
        xc = jnp.concatenate([halo_ref[0, :, sl], x], axis=0)
        w = w_ref[:, sl]
        acc = x * w[DN_CONV - 1:DN_CONV]
        for k in range(1, DN_CONV):
            shifted = pltpu.roll(xc, k, 0)[SUBLANE:]
            acc = acc + shifted * w[DN_CONV - 1 - k:DN_CONV - k]
        c = jax.nn.silu(acc)
        if h < 2 * DN_HK:
            c = c * lax.rsqrt(jnp.sum(c * c, -1, keepdims=True) + L2_EPS)
        if h < DN_HK:
            c = c * DN_DK ** -0.5
        o_ref[h] = c


def dn_conv(x, halo, conv_w, rows, tt):
    return pl.pallas_call(
        _dn_conv_body,
        out_shape=jax.ShapeDtypeStruct((DN_HEADS_ALL, rows, DN_DK), F32),
        grid=(rows // tt,),
        in_specs=[pl.BlockSpec((tt, DN_CONV_DIM), lambda i: (i, 0)),
                  pl.BlockSpec((1, SUBLANE, DN_CONV_DIM), lambda i: (i, 0, 0)),
                  _resident((DN_CONV, DN_CONV_DIM))],
        out_specs=pl.BlockSpec((DN_HEADS_ALL, tt, DN_DK), lambda i: (0, i, 0)),
        compiler_params=_cparams(("parallel",)),
        name="dn_conv",
    )(x, halo, conv_w)


def _bdot(a, b):
    return lax.dot_general(a, b, (((2,), (1,)), ((0,), (0,))), preferred_element_type=F32)


def _bdot_nt(a, b):
    return lax.dot_general(a, b, (((2,), (2,)), ((0,), (0,))), preferred_element_type=F32)


def _bdot3(a, b):
    ah, al = _split_bf16(a)
    bh, bl = _split_bf16(b)
    return _bdot(ah, bh) + (_bdot(ah, bl) + _bdot(al, bh))


def _delta_prep_body(q_ref, k_ref, v_ref, g_ref, b_ref, u_ref, wq_ref, qk_ref, kt_ref, gl_ref, *, hb):
    C = DN_CHUNK
    ri = lax.broadcasted_iota(jnp.int32, (C, C), 0)
    ci = lax.broadcasted_iota(jnp.int32, (C, C), 1)
    tri = ri >= ci
    stri = ri > ci
    eye = jnp.where(ri == ci, 1.0, 0.0).astype(F32)
    bi, bj = ri // INV_BLOCK, ci // INV_BLOCK
    rep = DN_HV // DN_HK
    nq = hb // rep
    q = q_ref[...][:, None]
    k = k_ref[...][:, None]
    kb16 = k_ref[...].astype(BF16)
    kk = _bdot_nt(kb16, kb16)[:, None]
    qk0 = _bdot_nt(q_ref[...].astype(BF16), kb16)[:, None]
    grow = g_ref[...].reshape(nq, rep, 1, C)
    brow = b_ref[...].reshape(nq, rep, 1, C)
    gcol = jnp.sum(eye * grow, -1, keepdims=True)
    bcol = jnp.sum(eye * brow, -1, keepdims=True)
    gc_col = jnp.sum(jnp.where(tri, grow, 0.0), -1, keepdims=True)
    gc_row = jnp.sum(jnp.where(ri <= ci, gcol, 0.0), -2, keepdims=True)
    decay = jnp.exp(jnp.where(tri, gc_col - gc_row, NEG))
    a_mat = (jnp.where(stri, kk * decay, 0.0) * bcol).reshape(hb, C, C)
    eg = jnp.exp(gc_col)
    v = v_ref[...].reshape(nq, rep, C, DN_DV)
    rhs = jnp.concatenate([v * bcol, k * (bcol * eg)], axis=-1).reshape(hb, C, 2 * DN_DV)
    p = -jnp.where(bi == bj, a_mat, 0.0)
    t = eye + p
    for _ in range(int(math.log2(INV_BLOCK)) - 1):
        p = _bdot3(p, p)
        t = t + _bdot3(t, p)
    width = INV_BLOCK
    while width < C:
        e = jnp.where((bi // (2 * width // INV_BLOCK) == bj // (2 * width // INV_BLOCK))
                      & (bi // (width // INV_BLOCK) != bj // (width // INV_BLOCK)), a_mat, 0.0)
        t = t - _bdot3(_bdot3(t, e), t)
        width *= 2
    sol = _bdot3(t, rhs)
    g_last = gc_col[:, :, C - 1:C]
    u_ref[...] = sol[:, :, :DN_DV]
    qg = (q * eg).reshape(hb, C, DN_DK)
    wq_ref[...] = jnp.concatenate([sol[:, :, DN_DV:], qg], axis=1).astype(BF16)
    qk_ref[...] = jnp.where(tri, qk0 * decay, 0.0).reshape(hb, C, C).astype(BF16)
    kdec = (k * jnp.exp(g_last - gc_col)).reshape(hb, C, DN_DK)
    kt_ref[...] = jnp.swapaxes(kdec, 1, 2).astype(BF16)
    gl_ref[...] = jnp.broadcast_to(jnp.exp(g_last), (nq, rep, 1, DN_DV)).reshape(hb, 1, 1, DN_DV)


def delta_prep(qkv_hm, g_rows, b_rows, B, n):
    hb = DN_HV
    rep = DN_HV // DN_HK
    C = DN_CHUNK
    nq = hb // rep
    bn = B * n
    row_blk = lambda rows, cols: pl.BlockSpec((hb, rows, cols), lambda c, h: (h, c, 0))
    return pl.pallas_call(
        functools.partial(_delta_prep_body, hb=hb),
        out_shape=(jax.ShapeDtypeStruct((DN_HV, bn * C, DN_DV), F32),
                   jax.ShapeDtypeStruct((DN_HV, bn * 2 * C, DN_DK), BF16),
                   jax.ShapeDtypeStruct((DN_HV, bn * C, C), BF16),
                   jax.ShapeDtypeStruct((DN_HV, bn * DN_DK, C), BF16),
                   jax.ShapeDtypeStruct((DN_HV, bn, 1, DN_DV), F32)),
        grid=(bn, DN_HV // hb),
        in_specs=[
            pl.BlockSpec((nq, C, DN_DK), lambda c, h: (h, c, 0)),
            pl.BlockSpec((nq, C, DN_DK), lambda c, h: (DN_HK // nq + h, c, 0)),
            pl.BlockSpec((hb, C, DN_DV), lambda c, h: (2 * DN_HK // hb + h, c, 0)),
            pl.BlockSpec((hb, 1, 1, C), lambda c, h: (h, c, 0, 0)),
            pl.BlockSpec((hb, 1, 1, C), lambda c, h: (h, c, 0, 0)),
        ],
        out_specs=(row_blk(C, DN_DV), row_blk(2 * C, DN_DK), row_blk(C, C), row_blk(DN_DK, C),
                   pl.BlockSpec((hb, 1, 1, DN_DV), lambda c, h: (h, c, 0, 0))),
        compiler_params=_cparams(("parallel", "parallel")),
        name="delta_prep",
    )(qkv_hm, qkv_hm, qkv_hm, g_rows, b_rows)


def _delta_scan_body(u_ref, wq_ref, qk_ref, kt_ref, gl_ref, s0_ref, o_ref, sout_ref, S_ref, *, hs):
    c = pl.program_id(2)
    C = DN_CHUNK

    @pl.when(c == 0)
    def _():
        S_ref[...] = s0_ref[0]

    S = S_ref[...]
    ws = _bdot(wq_ref[...], S.astype(BF16))
    vb = (u_ref[...] - ws[:, :C]).astype(BF16)
    o_ref[...] = ws[:, C:] + _bdot(qk_ref[...], vb)
    S_ref[...] = S * gl_ref[...][:, 0] + _bdot(kt_ref[...], vb)

    @pl.when(c == pl.num_programs(2) - 1)
    def _():
        sout_ref[0] = S_ref[...]


def delta_scan(u, wq, qk, kt, gl, s0, B, n):
    hs = DN_HV
    C = DN_CHUNK
    row_blk = lambda rows, cols: pl.BlockSpec((hs, rows, cols), lambda b, h, c: (h, b * n + c, 0))
    state_blk = pl.BlockSpec((1, hs, DN_DK, DN_DV), lambda b, h, c: (b, h, 0, 0))
    return pl.pallas_call(
        functools.partial(_delta_scan_body, hs=hs),
        out_shape=(jax.ShapeDtypeStruct((DN_HV, B * n * C, DN_DV), F32),
                   jax.ShapeDtypeStruct((B, DN_HV, DN_DK, DN_DV), F32)),
        grid=(B, DN_HV // hs, n),
        in_specs=[row_blk(C, DN_DV), row_blk(2 * C, DN_DK), row_blk(C, C), row_blk(DN_DK, C),
                  pl.BlockSpec((hs, 1, 1, DN_DV), lambda b, h, c: (h, b * n + c, 0, 0)), state_blk],
        out_specs=(row_blk(C, DN_DV), state_blk),
        scratch_shapes=[pltpu.VMEM((hs, DN_DK, DN_DV), F32)],
        compiler_params=_cparams(("parallel", "parallel", "arbitrary")),
        name="delta_scan",
    )(u, wq, qk, kt, gl, s0)


def _dn_out_body(o_ref, z_ref, g_ref, w_ref, r_ref, out_ref):
    parts = []
    for h in range(DN_HV):
        z = z_ref[:, h * DN_DV:(h + 1) * DN_DV]
        parts.append((_rms(o_ref[h], g_ref[...]) * jax.nn.silu(z)).astype(BF16))
    out_ref[...] = r_ref[...] + _dot(jnp.concatenate(parts, axis=-1), w_ref[...])


def dn_out(o_hm, proj, norm_g, w_out, res):
    T, D = res.shape
    tm = _pick(T, (256, 128, 64, 32, 16, 8))
    return pl.pallas_call(
        _dn_out_body,
        out_shape=jax.ShapeDtypeStruct((T, D), F32),
        grid=(T // tm,),
        in_specs=[pl.BlockSpec((DN_HV, tm, DN_DV), lambda i: (0, i, 0)),
                  pl.BlockSpec((tm, DN_VW), lambda i: (i, DN_CONV_DIM // DN_VW)),
                  _resident((1, DN_DV)), _resident((DN_VW, D)),
                  pl.BlockSpec((tm, D), lambda i: (i, 0))],
        out_specs=pl.BlockSpec((tm, D), lambda i: (i, 0)),
        compiler_params=_cparams(("parallel",)),
        name="dn_out",
    )(o_hm, proj, norm_g.reshape(1, DN_DV), w_out, res)


def _delta_group(h, proj, B, L, conv_buf, s0, conv_w, a_log, dt_bias, norm_g, w_out):
    C = DN_CHUNK
    Lp = _round_up(L, C)
    n = Lp // C
    qkv = proj[:, :DN_CONV_DIM].reshape(B, L, DN_CONV_DIM)
    new_buf = jnp.concatenate([conv_buf, qkv[:, max(L - (DN_CONV - 1), 0):]], axis=1)[:, -(DN_CONV - 1):]
    tt = _pick(Lp, (256, 128, 64))
    nt = Lp // tt
    if Lp == L:
        x = proj
    else:
        x = jnp.pad(qkv, ((0, 0), (0, Lp - L), (0, 0))).reshape(B * Lp, DN_CONV_DIM)
    first = jnp.concatenate([jnp.zeros((B, 1, SUBLANE - (DN_CONV - 1), DN_CONV_DIM), F32), conv_buf[:, None]], axis=2)
    if nt > 1:
        tails = x[:, :DN_CONV_DIM].reshape(B, nt, tt, DN_CONV_DIM)[:, :nt - 1, tt - SUBLANE:]
        halo = jnp.concatenate([first, tails], axis=1)
    else:
        halo = first
    qkv_hm = dn_conv(x, halo.reshape(B * nt, SUBLANE, DN_CONV_DIM), conv_w, B * Lp, tt)
    beta, g = dn_gates(proj, a_log, dt_bias)

    def rows(a):
        a = jnp.pad(a.reshape(B, L, DN_HV), ((0, 0), (0, Lp - L), (0, 0)))
        return a.reshape(B, n, C, DN_HV).transpose(3, 0, 1, 2).reshape(DN_HV, B * n, 1, C)

    o_hm, S = delta_scan(*delta_prep(qkv_hm, rows(g), rows(beta), B, n), s0, B, n)
    if Lp != L:
        o_hm = o_hm.reshape(DN_HV, B, Lp, DN_DV)[:, :, :L].reshape(DN_HV, B * L, DN_DV)
    return dn_out(o_hm, proj, norm_g, w_out, h), S, new_buf


def _sg_body(y_ref, lng_ref, lnb_ref, wsp_ref, bsp_ref, a_ref, *v_out, width):
    u = y_ref[:, :width]
    v = y_ref[:, width:]
    xc = v - jnp.mean(v, -1, keepdims=True)
    vn = xc * lax.rsqrt(jnp.mean(xc * xc, -1, keepdims=True) + LN_EPS) * lng_ref[...] + lnb_ref[...]
    if v_out:
        v_out[0][...] = vn
    n = wsp_ref.shape[1]
    gw = width // SG_GROUPS
    tril = lax.broadcasted_iota(jnp.int32, (n, n), 0) >= lax.broadcasted_iota(jnp.int32, (n, n), 1)
    for g in range(SG_GROUPS):
        sl = slice(g * gw, (g + 1) * gw)
        wt = jnp.where(tril, wsp_ref[g], 0.0).astype(BF16)
        mixed = _dot(wt, vn[:, sl].astype(BF16)) + bsp_ref[:, g:g + 1]
        a_ref[:, sl] = u[:, sl] * mixed


def sg_mix(y, ln_g, ln_b, w_sp, b_sp, want_v):
    T = y.shape[0]
    W = y.shape[1] // 2
    n = SG_CHUNK
    out_shape = [jax.ShapeDtypeStruct((T, W), F32)]
    out_specs = [pl.BlockSpec((n, W), lambda i: (i, 0))]
    if want_v:
        out_shape.append(jax.ShapeDtypeStruct((T, W), F32))
        out_specs.append(pl.BlockSpec((n, W), lambda i: (i, 0)))
    return pl.pallas_call(
        functools.partial(_sg_body, width=W),
        out_shape=tuple(out_shape),
        grid=(T // n,),
        in_specs=[pl.BlockSpec((n, 2 * W), lambda i: (i, 0)), _resident((1, W)), _resident((1, W)),
                  _resident((SG_GROUPS, n, n)), _resident((n, SG_GROUPS))],
        out_specs=tuple(out_specs),
        compiler_params=_cparams(("parallel",)),
        name="sg_mix",
    )(y, ln_g.reshape(1, W), ln_b.reshape(1, W), w_sp, b_sp.T)


def _norm_rope_body(x_ref, g_ref, cos_ref, sin_ref, *out_refs, want_norm):
    xn = _rms(x_ref[0], g_ref[...])
    half = NSA_DH // 2
    xr = jnp.concatenate([xn[:, half:], xn[:, :half]], axis=-1)
    rot = xn * cos_ref[...] + xr * sin_ref[...]
    if want_norm:
        out_refs[0][0] = xn
    out_refs[-1][0] = rot


def norm_rope(x, g, cos, sin, want_norm=False):
    R, L, dh = x.shape
    tl = _pick(L, (1024, 512, 256, 128, 64, 32, 16, 8, L))
    blk = pl.BlockSpec((1, tl, dh), lambda r, l: (r, l, 0))
    n_out = 2 if want_norm else 1
    outs = pl.pallas_call(
        functools.partial(_norm_rope_body, want_norm=want_norm),
        out_shape=(jax.ShapeDtypeStruct((R, L, dh), F32),) * n_out,
        grid=(R, L // tl),
        in_specs=[blk, _resident((1, dh)), pl.BlockSpec((tl, dh), lambda r, l: (l, 0)),
                  pl.BlockSpec((tl, dh), lambda r, l: (l, 0))],
        out_specs=(blk,) * n_out,
        compiler_params=_cparams(("parallel", "parallel")),
        name="norm_rope",
    )(x, g.reshape(1, dh), cos, sin)
    return outs if want_norm else outs[0]


def _cmp_proj_body(x_ref, w_ref, o_ref):
    o_ref[...] = _dot(x_ref[...].astype(BF16), w_ref[...])


def cmp_proj(x, wbig):
    rows = x.shape[0]
    tm = _pick(rows, (512, 256, 128, 64, 32, 16, 8))
    return pl.pallas_call(
        _cmp_proj_body,
        out_shape=jax.ShapeDtypeStruct((rows, 2 * NSA_KW), F32),
        grid=(rows // tm,),
        in_specs=[pl.BlockSpec((tm, SEG_W), lambda i: (i, 0)), _resident((SEG_W, 2 * NSA_KW))],
        out_specs=pl.BlockSpec((tm, 2 * NSA_KW), lambda i: (i, 0)),
        compiler_params=_cparams(("parallel",)),
        name="cmp_proj",
    )(x, wbig)


def _page_copy(pool_ref, buf_ref, sem_ref, page, slot, p):
    return pltpu.make_async_copy(pool_ref.at[page], buf_ref.at[slot, p], sem_ref.at[slot])


def _fetch_pages(pt_ref, pool_ref, buf_ref, sem_ref, b, chunk, slot, pch):
    for p in range(pch):
        _page_copy(pool_ref, buf_ref, sem_ref, pt_ref[b, chunk * pch + p], slot, p).start()


def _wait_pages(pool_ref, buf_ref, sem_ref, slot, pch):
    for p in range(pch):
        _page_copy(pool_ref, buf_ref, sem_ref, 0, slot, p).wait()


def _cmp_proj_paged_body(pt_ref, pool_ref, w_ref, o_ref, buf_ref, sem_ref, *, pch):
    b, ch = pl.program_id(0), pl.program_id(1)
    nb, nch = pl.num_programs(0), pl.num_programs(1)
    step = b * nch + ch
    slot = lax.rem(step, 2)

    @pl.when(step == 0)
    def _():
        _fetch_pages(pt_ref, pool_ref, buf_ref, sem_ref, b, ch, slot, pch)

    @pl.when(step + 1 < nb * nch)
    def _():
        nxt = step + 1
        _fetch_pages(pt_ref, pool_ref, buf_ref, sem_ref, nxt // nch, lax.rem(nxt, nch), 1 - slot, pch)

    _wait_pages(pool_ref, buf_ref, sem_ref, slot, pch)
    x = buf_ref[slot].reshape(pch * buf_ref.shape[2], SEG_W)
    o_ref[...] = _dot(x.astype(BF16), w_ref[...])


def cmp_proj_paged(page_table, pool, wbig):
    B, n_pages = page_table.shape
    segs = pool.shape[1]
    pch = _pick(n_pages, (32, 16, 8, 4, 2, 1))
    nch = n_pages // pch
    return pl.pallas_call(
        functools.partial(_cmp_proj_paged_body, pch=pch),
        out_shape=jax.ShapeDtypeStruct((B * n_pages * segs, 2 * NSA_KW), F32),
        grid_spec=pltpu.PrefetchScalarGridSpec(
            num_scalar_prefetch=1,
            grid=(B, nch),
            in_specs=[pl.BlockSpec(memory_space=pl.ANY),
                      pl.BlockSpec((SEG_W, 2 * NSA_KW), lambda b, c, pt: (0, 0))],
            out_specs=pl.BlockSpec((pch * segs, 2 * NSA_KW), lambda b, c, pt: (b * nch + c, 0)),
            scratch_shapes=[pltpu.VMEM((2, pch, segs, SEG_W), F32), pltpu.SemaphoreType.DMA((2,))],
        ),
        compiler_params=_cparams(("arbitrary", "arbitrary")),
        name="cmp_proj_paged",
    )(page_table, pool, wbig)


def _cmp_finish_body(a_ref, pe_ref, w_ref, g_ref, o_ref, *, do_norm):
    a = a_ref[0]
    n = a.shape[0]
    pe = _dot(pe_ref[...].astype(BF16), w_ref[...])
    pe_term = pe[0:1, :NSA_KW] + pe[1:2, NSA_KW:]
    blk = a[:, :NSA_KW] + pltpu.roll(a[:, NSA_KW:], n - 1, 0) + pe_term
    for h in range(NSA_HKV):
        bh = blk[:, h * NSA_DH:(h + 1) * NSA_DH]
        if do_norm:
            bh = _rms(bh, g_ref[...])
        o_ref[0, h] = bh


def cmp_finish(a, pe_rows, wbig, g, B, n, do_norm):
    return pl.pallas_call(
        functools.partial(_cmp_finish_body, do_norm=do_norm),
        out_shape=jax.ShapeDtypeStruct((B, NSA_HKV, n, NSA_DH), F32),
        grid=(B,),
        in_specs=[pl.BlockSpec((1, n, 2 * NSA_KW), lambda b: (b, 0, 0)), _resident((SUBLANE, SEG_W)),
                  _resident((SEG_W, 2 * NSA_KW)), _resident((1, NSA_DH))],
        out_specs=pl.BlockSpec((1, NSA_HKV, n, NSA_DH), lambda b: (b, 0, 0, 0)),
        compiler_params=_cparams(("parallel",)),
        name="cmp_finish",
    )(a.reshape(B, n, 2 * NSA_KW), pe_rows, wbig, g.reshape(1, NSA_DH))


def _masked_softmax(s, mask):
    sm = jnp.where(mask, s, NEG)
    m = jnp.max(sm, -1, keepdims=True)
    p = jnp.where(mask, jnp.exp(s - m), 0.0)
    d = jnp.sum(p, -1, keepdims=True)
    return p / jnp.where(d > 0, d, 1.0)


def _topk_mask(score, k):
    nl = score.shape[-1]
    lane = lax.broadcasted_iota(jnp.int32, score.shape, 1).astype(F32)
    sel = jnp.zeros(score.shape, F32)
    work = score
    for _ in range(k):
        m = jnp.max(work, -1, keepdims=True)
        idx = jnp.min(jnp.where(work == m, lane, float(nl)), -1, keepdims=True)
        hit = lane == idx
        sel = jnp.where(hit, 1.0, sel)
        work = jnp.where(hit, -jnp.inf, work)
    return sel


def _select_blocks(imp, qpos, n_lanes_valid):
    blk = lax.broadcasted_iota(jnp.int32, imp.shape, 1)
    cur = qpos // SLC_BLOCK
    eligible = (blk * SLC_BLOCK <= qpos) & (blk < n_lanes_valid)
    forced = (blk == 0) | (blk == cur) | (blk == cur - 1)
    score = jnp.where(eligible, imp + jnp.where(forced, FORCE_BONUS, 0.0), -jnp.inf)
    return _topk_mask(score, N_SEL)


def _flash_tile(q3, k, v, mask, carry):
    return _flash_tile_bias(q3, k, v, jnp.where(mask, 0.0, NEG), carry)


def _flash_tile_bias(q3, k, v, bias, carry):
    m, l, acc = carry
    G, R, dh = q3.shape
    tk = k.shape[0]
    s = _dot_nt(q3.reshape(G * R, dh), k).reshape(G, R, tk) + bias[None]
    m_new = jnp.maximum(m, jnp.max(s, -1, keepdims=True))
    alpha = jnp.exp2(m - m_new)
    p = jnp.exp2(s - m_new)
    l = alpha * l + jnp.sum(p, -1, keepdims=True)
    pv = _dot(p.reshape(G * R, tk).astype(BF16), v).reshape(G, R, dh)
    return m_new, l, alpha * acc + pv


def _flash_init(G, R, dh):
    return (jnp.full((G, R, 1), NEG, F32), jnp.zeros((G, R, 1), F32), jnp.zeros((G, R, dh), F32))


def _flash_done(carry):
    m, l, acc = carry
    return jnp.where(m > 0.5 * NEG, acc / l, 0.0)


def _nsa_attn_body(qn_ref, q_ref, kc_ref, vc_ref, m_ref, ks_ref, vs_ref, kw_ref, vw_ref, gt_ref, o_ref, *, tk, n_slc):
    i = pl.program_id(2)
    qb = q_ref.shape[2]
    nb = m_ref.shape[1]
    rows = NSA_G * qb
    qpos = i * qb + lax.broadcasted_iota(jnp.int32, (qb, 1), 0)

    kc = kc_ref[0, 0]
    ncp = kc.shape[0]
    s = _dot3_nt(qn_ref[0].reshape(rows, NSA_DH) * NSA_DH ** -0.5, kc).reshape(NSA_G, qb, ncp)
    kend = lax.broadcasted_iota(jnp.int32, (1, ncp), 1) * CMP_STRIDE + (CMP_BLOCK - 1)
    p = _masked_softmax(s, (kend <= qpos)[None])
    o_cmp = _dot(p.reshape(rows, ncp).astype(BF16), vc_ref[0, 0].astype(BF16)).reshape(NSA_G, qb, NSA_DH)
    psum = p[0]
    for g in range(1, NSA_G):
        psum = psum + p[g]
    ps_hi, ps_lo = _split_bf16(psum)
    overlap = m_ref[...].astype(BF16)
    sel = _select_blocks(_dot(ps_hi, overlap) + _dot(ps_lo, overlap), qpos, n_slc)
    sel_bias = jnp.where(sel > 0.5, 0.0, NEG)

    q3 = (q_ref[0] * QK_SCALE_LOG2).astype(BF16)
    carry = _flash_init(NSA_G, qb, NSA_DH)
    lane = lax.broadcasted_iota(jnp.int32, (1, qb), 1)
    for t in range(WINDOW // qb + 1):
        jt = i - WINDOW // qb + t
        start = pl.multiple_of(jnp.maximum(jt, 0) * qb, qb)
        kpos = jt * qb + lane
        d = qpos - kpos
        mask = (d >= 0) & (d <= WINDOW) & (kpos >= 0)
        carry = _flash_tile(q3, kw_ref[0, 0, pl.ds(start, qb), :], vw_ref[0, 0, pl.ds(start, qb), :], mask, carry)
    o_swa = _flash_done(carry)

    bpt = tk // SLC_BLOCK
    expand = jnp.where(lax.broadcasted_iota(jnp.int32, (nb, tk), 0)
                       == lax.broadcasted_iota(jnp.int32, (nb, tk), 1) // SLC_BLOCK, 1.0, 0.0).astype(BF16)

    def key_bias(j):
        rolled = pltpu.roll(sel_bias, lax.rem(nb - j * bpt, nb), 1)
        return _dot(rolled.astype(BF16), expand)

    def slc_step(j, carry):
        start = pl.multiple_of(j * tk, tk)
        return _flash_tile_bias(q3, ks_ref[0, 0, pl.ds(start, tk), :], vs_ref[0, 0, pl.ds(start, tk), :],
                                key_bias(j), carry)

    assert tk % qb == 0
    n_full = (i * qb) // tk
    carry = lax.fori_loop(0, n_full, slc_step, _flash_init(NSA_G, qb, NSA_DH))
    start = pl.multiple_of(n_full * tk, tk)
    kpos = n_full * tk + lax.broadcasted_iota(jnp.int32, (1, tk), 1)
    diag_bias = key_bias(n_full) + jnp.where(kpos <= qpos, 0.0, NEG)
    o_slc = _flash_done(_flash_tile_bias(q3, ks_ref[0, 0, pl.ds(start, tk), :], vs_ref[0, 0, pl.ds(start, tk), :],
                                         diag_bias, carry))

    gt = jax.nn.sigmoid(gt_ref[0, 0])
    parts = []
    for g in range(NSA_G):
        parts.append(gt[:, 3 * g:3 * g + 1] * o_cmp[g] + gt[:, 3 * g + 1:3 * g + 2] * o_slc[g]
                     + gt[:, 3 * g + 2:3 * g + 3] * o_swa[g])
    o_ref[0] = jnp.concatenate(parts, axis=-1)


def nsa_attn(qn, qr, kcmp, vcmp, m_mat, ks, vs, kw, vw, gates, n_slc):
    B, _, L, dh = qr.shape
    ncp = kcmp.shape[2]
    nb = m_mat.shape[1]
    qb = Q_BLOCK
    tk = _pick(L, (1024, 512, 256, 128))
    q_spec = pl.BlockSpec((1, NSA_G, qb, dh), lambda b, h, i: (b, h, i, 0))
    cmp_spec = pl.BlockSpec((1, 1, ncp, dh), lambda b, h, i: (b, h, 0, 0))
    kv_spec = pl.BlockSpec((1, 1, L, dh), lambda b, h, i: (b, h, 0, 0))
    return pl.pallas_call(
        functools.partial(_nsa_attn_body, tk=tk, n_slc=n_slc),
        out_shape=jax.ShapeDtypeStruct((B, L, NSA_QW), F32),
        grid=(B, NSA_HKV, L // qb),
        in_specs=[q_spec, q_spec, cmp_spec, cmp_spec, _resident((ncp, nb)), kv_spec, kv_spec, kv_spec, kv_spec,
                  pl.BlockSpec((1, 1, qb, 3 * NSA_G), lambda b, h, i: (b, h, i, 0))],
        out_specs=pl.BlockSpec((1, qb, NSA_G * dh), lambda b, h, i: (b, i, h)),
        compiler_params=_cparams(("parallel", "parallel", "parallel")),
        name="nsa_attn",
    )(qn, qr, kcmp, vcmp, m_mat, ks, vs, kw, vw, gates)


def _sample_attn_body(pt_ref, qn_ref, qr_ref, kc_ref, vc_ref, kpool_ref, vpool_ref, ksn_ref, vsn_ref,
                      bk_ref, bv_ref, kwn_ref, vwn_ref, gt_ref, m_ref, o_ref,
                      kbuf, vbuf, ksem, vsem, sel_s, m_s, l_s, acc_s, *, pch, L, pos0, n_slc):
    b, ch = pl.program_id(0), pl.program_id(1)
    nb, nch = pl.num_programs(0), pl.num_programs(1)
    step = b * nch + ch
    slot = lax.rem(step, 2)
    R = NSA_G * L
    page = kbuf.shape[2]
    tk = pch * page

    @pl.when(step == 0)
    def _():
        _fetch_pages(pt_ref, kpool_ref, kbuf, ksem, b, ch, slot, pch)
        _fetch_pages(pt_ref, vpool_ref, vbuf, vsem, b, ch, slot, pch)

    @pl.when(step + 1 < nb * nch)
    def _():
        nxt = step + 1
        _fetch_pages(pt_ref, kpool_ref, kbuf, ksem, nxt // nch, lax.rem(nxt, nch), 1 - slot, pch)
        _fetch_pages(pt_ref, vpool_ref, vbuf, vsem, nxt // nch, lax.rem(nxt, nch), 1 - slot, pch)

    row = lax.broadcasted_iota(jnp.int32, (R, 1), 0)
    tq = row & (L - 1)
    qpos = pos0 + tq
    same_t = jnp.where((lax.broadcasted_iota(jnp.int32, (R, R), 0) & (L - 1))
                       == (lax.broadcasted_iota(jnp.int32, (R, R), 1) & (L - 1)), 1.0, 0.0)
    scale = NSA_DH ** -0.5

    @pl.when(ch == 0)
    def _():
        for h in range(NSA_HKV):
            kc = kc_ref[0, h]
            ncp = kc.shape[0]
            s = _dot_nt(qn_ref[0, h] * scale, kc, HIGHEST)
            kend = lax.broadcasted_iota(jnp.int32, (1, ncp), 1) * CMP_STRIDE + (CMP_BLOCK - 1)
            p = _masked_softmax(s, kend <= qpos)
            acc_s[0, h] = _dot(p.astype(BF16), vc_ref[0, h].astype(BF16))
            imp = _dotf(_dotf(same_t, p), m_ref[...])
            sel_s[h] = _select_blocks(imp, qpos, n_slc)
            m_s[h] = jnp.full((R, 1), NEG, F32)
            l_s[h] = jnp.zeros((R, 1), F32)
            acc_s[1, h] = jnp.zeros((R, NSA_DH), F32)

    _wait_pages(kpool_ref, kbuf, ksem, slot, pch)
    _wait_pages(vpool_ref, vbuf, vsem, slot, pch)
    nbl = sel_s.shape[2]
    blk_i = lax.broadcasted_iota(jnp.int32, (nbl, tk), 0)
    key_blk = lax.broadcasted_iota(jnp.int32, (nbl, tk), 1) // SLC_BLOCK + ch * (tk // SLC_BLOCK)
    expand = jnp.where(blk_i == key_blk, 1.0, 0.0).astype(BF16)
    kall = kbuf[slot].reshape(tk, NSA_KW)
    vall = vbuf[slot].reshape(tk, NSA_KW)
    for h in range(NSA_HKV):
        sl = slice(h * NSA_DH, (h + 1) * NSA_DH)
        q3 = (qr_ref[0, h] * QK_SCALE_LOG2).astype(BF16)[None]
        mask = _dot(sel_s[h].astype(BF16), expand) > 0.5
        carry = (m_s[h][None], l_s[h][None], acc_s[1, h][None])
        m, l, acc = _flash_tile(q3, kall[:, sl].astype(BF16), vall[:, sl].astype(BF16), mask, carry)
        m_s[h], l_s[h], acc_s[1, h] = m[0], l[0], acc[0]

    @pl.when(ch == nch - 1)
    def _():
        npad = ksn_ref.shape[1]
        tn = lax.broadcasted_iota(jnp.int32, (1, npad), 1)
        new_ok = (tn <= tq) & (tn < L)
        wb = bk_ref.shape[1]
        wpos = pos0 - wb + lax.broadcasted_iota(jnp.int32, (1, wb), 1)
        dw = qpos - wpos
        win_ok = (dw >= 0) & (dw <= WINDOW)
        cur_blk = pos0 // SLC_BLOCK
        for h in range(NSA_HKV):
            sl = slice(h * NSA_DH, (h + 1) * NSA_DH)
            q3 = (qr_ref[0, h] * QK_SCALE_LOG2).astype(BF16)[None]
            sel_new = sel_s[h][:, cur_blk:cur_blk + 1] > 0.5
            carry = (m_s[h][None], l_s[h][None], acc_s[1, h][None])
            carry = _flash_tile(q3, ksn_ref[0][:, sl].astype(BF16), vsn_ref[0][:, sl].astype(BF16),
                                new_ok & sel_new, carry)
            o_slc = _flash_done(carry)[0]
            carry = _flash_init(1, R, NSA_DH)
            carry = _flash_tile(q3, bk_ref[0][:, sl].astype(BF16), bv_ref[0][:, sl].astype(BF16), win_ok, carry)
            carry = _flash_tile(q3, kwn_ref[0][:, sl].astype(BF16), vwn_ref[0][:, sl].astype(BF16), new_ok, carry)
            o_swa = _flash_done(carry)[0]
            gt = jax.nn.sigmoid(gt_ref[0, h])
            o_ref[0, h] = gt[:, 0:1] * acc_s[0, h] + gt[:, 1:2] * o_slc + gt[:, 2:3] * o_swa


def sample_attn(page_table, qn, qr, kcmp, vcmp, kpool, vpool, ks_new, vs_new, buf_k, buf_v, kw_new, vw_new,
                gates, m_mat, L, pos0, n_slc):
    B, n_pages = page_table.shape
    page = kpool.shape[1]
    R = NSA_G * L
    pch = _pick(n_pages, (16, 8, 4, 2, 1))
    nch = n_pages // pch
    nbl = m_mat.shape[1]
    ncp = kcmp.shape[2]
    assert L & (L - 1) == 0 and pos0 % SLC_BLOCK == 0 and pos0 // SLC_BLOCK < nbl
    per_b = lambda *blk: pl.BlockSpec((1,) + blk, lambda b, c, pt: (b,) + (0,) * len(blk))
    return pl.pallas_call(
        functools.partial(_sample_attn_body, pch=pch, L=L, pos0=pos0, n_slc=n_slc),
        out_shape=jax.ShapeDtypeStruct((B, NSA_HKV, R, NSA_DH), F32),
        grid_spec=pltpu.PrefetchScalarGridSpec(
            num_scalar_prefetch=1,
            grid=(B, nch),
            in_specs=[per_b(NSA_HKV, R, NSA_DH), per_b(NSA_HKV, R, NSA_DH),
                      per_b(NSA_HKV, ncp, NSA_DH), per_b(NSA_HKV, ncp, NSA_DH),
                      pl.BlockSpec(memory_space=pl.ANY), pl.BlockSpec(memory_space=pl.ANY),
                      per_b(SUBLANE, NSA_KW), per_b(SUBLANE, NSA_KW),
                      per_b(buf_k.shape[1], NSA_KW), per_b(buf_k.shape[1], NSA_KW),
                      per_b(SUBLANE, NSA_KW), per_b(SUBLANE, NSA_KW),
                      per_b(NSA_HKV, R, 3),
                      pl.BlockSpec((ncp, nbl), lambda b, c, pt: (0, 0))],
            out_specs=per_b(NSA_HKV, R, NSA_DH),
            scratch_shapes=[pltpu.VMEM((2, pch, page, NSA_KW), F32), pltpu.VMEM((2, pch, page, NSA_KW), F32),
                            pltpu.SemaphoreType.DMA((2,)), pltpu.SemaphoreType.DMA((2,)),
                            pltpu.VMEM((NSA_HKV, R, nbl), F32), pltpu.VMEM((NSA_HKV, R, 1), F32),
                            pltpu.VMEM((NSA_HKV, R, 1), F32), pltpu.VMEM((2, NSA_HKV, R, NSA_DH), F32)],
        ),
        compiler_params=_cparams(("arbitrary", "arbitrary")),
        name="sample_attn",
    )(page_table, qn, qr, kcmp, vcmp, kpool, vpool, ks_new, vs_new, buf_k, buf_v, kw_new, vw_new, gates, m_mat)


def _cmp_to_slc(n_cmp_rows, n_cols, n_slc):
    k = np.arange(n_cmp_rows)[:, None] * CMP_STRIDE
    j = np.arange(n_cols)[None, :] * SLC_BLOCK
    ov = np.minimum(k + CMP_BLOCK, j + SLC_BLOCK) - np.maximum(k, j)
    m = np.clip(ov, 0, None).astype(np.float32) / CMP_BLOCK
    m[:, n_slc:] = 0.0
    return jnp.asarray(m)


def _rope_tables(pos0, L):
    half = NSA_DH // 2
    inv = ROPE_THETA ** (-jnp.arange(half, dtype=F32) / half)
    ang = (pos0 + jnp.arange(L)).astype(F32)[:, None] * inv[None, :]
    cos, sin = jnp.cos(ang), jnp.sin(ang)
    return jnp.concatenate([cos, cos], -1), jnp.concatenate([-sin, sin], -1)


def _cmp_weights(w, pe):
    w = w.reshape(2, CMP_STRIDE, NSA_DH, NSA_DH)
    eye = jnp.eye(NSA_HKV, dtype=F32)
    wbig = jnp.einsum("alde,hg->lhdage", w, eye).reshape(SEG_W, 2 * NSA_KW).astype(BF16)
    pe = pe.reshape(2, CMP_STRIDE, 1, NSA_DH)
    pe_rows = jnp.broadcast_to(pe, (2, CMP_STRIDE, NSA_HKV, NSA_DH)).reshape(2, SEG_W)
    return wbig, jnp.pad(pe_rows, ((0, SUBLANE - 2), (0, 0)))


def _nsa_split(proj, B, L):
    edges = [NSA_QW + i * NSA_KW for i in range(7)]
    cols = [proj[:, a:b] for a, b in zip([0] + edges, edges + [edges[-1] + 3 * NSA_HQ])]
    q, kc, vc, ks, vs, kw, vw, gate = cols
    tok = lambda a: a.reshape(B, L, NSA_HKV, NSA_DH)
    return q.reshape(B, L, NSA_HQ, NSA_DH), tok(kc), tok(vc), tok(ks), tok(vs), tok(kw), tok(vw), gate


def _head_major(a):
    B, L, H, dh = a.shape
    return a.transpose(0, 2, 1, 3).reshape(B * H, L, dh)


def _nsa_project(proj, B, L, pos0, q_norm, k_norm):
    q, kc, vc, ks, vs, kw, vw, gate = _nsa_split(proj, B, L)
    cos, sin = _rope_tables(pos0, L)
    qn, qr = norm_rope(_head_major(q), q_norm, cos, sin, want_norm=True)
    ksr = norm_rope(_head_major(ks), k_norm[1], cos, sin)
    kwr = norm_rope(_head_major(kw), k_norm[2], cos, sin)
    hm4 = lambda a, H: a.reshape(B, H, L, NSA_DH)
    tok = lambda a: hm4(a, NSA_HKV).transpose(0, 2, 1, 3)
    return hm4(qn, NSA_HQ), hm4(qr, NSA_HQ), gate, kc, vc, tok(ksr), vs, tok(kwr), vw, hm4(ksr, NSA_HKV), hm4(kwr, NSA_HKV)


def _nsa_prompt(h, proj, B, L, q_norm, k_norm, pe_k, pe_v, w_ck, w_cv, w_out):
    qn, qr, gate, kc, vc, ks, vs, kw, vw, ks_hm, kw_hm = _nsa_project(proj, B, L, 0, q_norm, k_norm)
    n_seg = L // CMP_STRIDE
    n_slc = L // SLC_BLOCK
    assert L % Q_BLOCK == 0 and n_slc <= LANE and n_slc >= N_SEL
    wk, pek = _cmp_weights(w_ck, pe_k)
    wv, pev = _cmp_weights(w_cv, pe_v)
    kcmp = cmp_finish(cmp_proj(kc.reshape(B * n_seg, SEG_W), wk), pek, wk, k_norm[0], B, n_seg, True)
    vcmp = cmp_finish(cmp_proj(vc.reshape(B * n_seg, SEG_W), wv), pev, wv, k_norm[0], B, n_seg, False)
    hm = lambda a: a.transpose(0, 2, 1, 3).astype(BF16)
    gates = gate.reshape(B, L, NSA_HKV, 3 * NSA_G).transpose(0, 2, 1, 3)
    o = nsa_attn(qn, qr, kcmp, vcmp, _cmp_to_slc(n_seg, LANE, n_slc), ks_hm.astype(BF16), hm(vs),
                 kw_hm.astype(BF16), hm(vw), gates, n_slc)
    n_keep = min(WINDOW, L)
    return mm_res(o.reshape(B * L, NSA_QW), w_out, h), (kc, vc, ks, vs, kw[:, L - n_keep:], vw[:, L - n_keep:])


def _nsa_sample(h, proj, B, L, past, page_table, q_norm, k_norm, pe_k, pe_v, w_ck, w_cv, w_out):
    pc_k, pc_v, ps_k, ps_v, buf_k, buf_v = past
    n_pool, page = pc_k.shape[:2]
    n_pages = page_table.shape[1]
    pos0 = n_pages * page
    qn, qr, gate, kc, vc, ks, vs, kw, vw, _, _ = _nsa_project(proj, B, L, pos0, q_norm, k_norm)
    assert L <= CMP_STRIDE and page % CMP_STRIDE == 0 and pos0 % SLC_BLOCK == 0
    n_seg = pos0 // CMP_STRIDE
    n_slc = -(-(pos0 + L) // SLC_BLOCK)
    nbl = _round_up(n_slc, LANE)
    wk, pek = _cmp_weights(w_ck, pe_k)
    wv, pev = _cmp_weights(w_cv, pe_v)
    seg_pool = lambda p: p.reshape(n_pool, page // CMP_STRIDE, SEG_W)
    kcmp = cmp_finish(cmp_proj_paged(page_table, seg_pool(pc_k), wk), pek, wk, k_norm[0], B, n_seg, True)
    vcmp = cmp_finish(cmp_proj_paged(page_table, seg_pool(pc_v), wv), pev, wv, k_norm[0], B, n_seg, False)
    rows = lambda a: a.reshape(B, NSA_HKV, NSA_G * L, NSA_DH)
    flat_pool = lambda p: p.reshape(n_pool, page, NSA_KW)
    new8 = lambda a: jnp.pad(a.reshape(B, L, NSA_KW), ((0, 0), (0, SUBLANE - L), (0, 0)))
    wb = buf_k.shape[1]
    gates = gate.reshape(B, L, NSA_HKV, NSA_G, 3).transpose(0, 2, 3, 1, 4).reshape(B, NSA_HKV, NSA_G * L, 3)
    o = sample_attn(page_table, rows(qn), rows(qr), kcmp, vcmp, flat_pool(ps_k), flat_pool(ps_v),
                    new8(ks), new8(vs), buf_k.reshape(B, wb, NSA_KW), buf_v.reshape(B, wb, NSA_KW),
                    new8(kw), new8(vw), gates, _cmp_to_slc(n_seg, nbl, n_slc), L, pos0, n_slc)
    o = o.reshape(B, NSA_HKV, NSA_G, L, NSA_DH).transpose(0, 3, 1, 2, 4).reshape(B * L, NSA_QW)
    win_k = jnp.concatenate([buf_k, kw], axis=1)[:, L:]
    win_v = jnp.concatenate([buf_v, vw], axis=1)[:, L:]
    return mm_res(o, w_out, h), (kc, vc, ks, vs, win_k, win_v)


def _pad_cols(w, n):
    return jnp.pad(w, ((0, 0), (0, n - w.shape[1]))).astype(BF16)


def kernel(x_prompt, x_sample, state_delta, state_conv, cache_swa_k, cache_swa_v, cache_cmp_k, cache_cmp_v, cache_slc_k, cache_slc_v, page_table, norm_mix, norm_ffn, dn_w_in, dn_conv_w, dn_a_log, dn_dt_bias, dn_norm, dn_w_out, sg_w_in, sg_ln_g, sg_ln_b, sg_w_spatial, sg_b_spatial, sg_w_out, nsa_w_in, nsa_q_norm, nsa_k_norm, nsa_cmp_pe_k, nsa_cmp_pe_v, nsa_cmp_w_k, nsa_cmp_w_v, nsa_w_out, ffn_w_gate, ffn_w_up, ffn_w_down):
    Bp, Lp, D = x_prompt.shape
    Bs, Ls, _ = x_sample.shape
    depth = norm_mix.shape[0]
    hp = x_prompt.reshape(Bp * Lp, D)
    hs = x_sample.reshape(Bs * Ls, D)
    dn_S_p, dn_S_s, dn_c_p, dn_c_s, sg_v_s = [], [], [], [], []
    c_p = ([], [], [], [], [], [])
    c_s = ([], [], [], [], [], [])
    for i in range(depth):
        kind, j = i % N_MIXERS, i // N_MIXERS
        if kind == 0:
            w_in = _pad_cols(dn_w_in[j], _round_up(dn_w_in.shape[2], LANE))
            w_out = dn_w_out[j].astype(BF16)
            prm = (dn_conv_w[j], dn_a_log[j], dn_dt_bias[j], dn_norm[j], w_out)
            proj_p = rms_matmul(hp, norm_mix[i], w_in)
            proj_s = rms_matmul(hs, norm_mix[i], w_in)
            zero_buf = jnp.zeros((Bp, DN_CONV - 1, DN_CONV_DIM), F32)
            zero_S = jnp.zeros((Bp, DN_HV, DN_DK, DN_DV), F32)
            hp, Sp, cp = _delta_group(hp, proj_p, Bp, Lp, zero_buf, zero_S, *prm)
            hs, Ss, cs = _delta_group(hs, proj_s, Bs, Ls, state_conv[j], state_delta[j], *prm)
            dn_S_p.append(Sp); dn_S_s.append(Ss); dn_c_p.append(cp); dn_c_s.append(cs)
        elif kind == 1:
            W = sg_w_in.shape[2] // 2
            w_in = sg_w_in[j].astype(BF16)
            w_out = sg_w_out[j].astype(BF16)
            prm = (sg_ln_g[j], sg_ln_b[j], sg_w_spatial[j], sg_b_spatial[j])
            y_p = rms_matmul(hp, norm_mix[i], w_in, act="gelu")
            y_s = rms_matmul(hs, norm_mix[i], w_in, act="gelu")
            (a_p,) = sg_mix(y_p, *prm, want_v=False)
            y_s = jnp.pad(y_s.reshape(Bs, Ls, 2 * W), ((0, 0), (0, SG_CHUNK - Ls), (0, 0)))
            a_s, v_s = sg_mix(y_s.reshape(Bs * SG_CHUNK, 2 * W), *prm, want_v=True)
            unpad = lambda a: a.reshape(Bs, SG_CHUNK, W)[:, :Ls]
            hp = mm_res(a_p, w_out, hp)
            hs = mm_res(unpad(a_s).reshape(Bs * Ls, W), w_out, hs)
            sg_v_s.append(unpad(v_s))
        else:
            w_in = _pad_cols(nsa_w_in[j], _round_up(nsa_w_in.shape[2], LANE))
            w_out = nsa_w_out[j].astype(BF16)
            prm = (nsa_q_norm[j], nsa_k_norm[j], nsa_cmp_pe_k[j], nsa_cmp_pe_v[j], nsa_cmp_w_k[j], nsa_cmp_w_v[j], w_out)
            past = (cache_cmp_k[j], cache_cmp_v[j], cache_slc_k[j], cache_slc_v[j], cache_swa_k[j], cache_swa_v[j])
            proj_p = rms_matmul(hp, norm_mix[i], w_in)
            proj_s = rms_matmul(hs, norm_mix[i], w_in)
            hp, new_p = _nsa_prompt(hp, proj_p, Bp, Lp, *prm)
            hs, new_s = _nsa_sample(hs, proj_s, Bs, Ls, past, page_table, *prm)
            for lst, a in zip(c_p, new_p):
                lst.append(a)
            for lst, a in zip(c_s, new_s):
                lst.append(a)
        wg, wu, wd = ffn_w_gate[i].astype(BF16), ffn_w_up[i].astype(BF16), ffn_w_down[i].astype(BF16)
        hp = ffn(hp, norm_ffn[i], wg, wu, wd)
        hs = ffn(hs, norm_ffn[i], wg, wu, wd)
    return (hp.reshape(Bp, Lp, D), hs.reshape(Bs, Ls, D),
            jnp.stack(dn_S_p), jnp.stack(dn_S_s), jnp.stack(dn_c_p), jnp.stack(dn_c_s),
            jnp.stack(sg_v_s),
            jnp.stack(c_p[4]), jnp.stack(c_p[5]), jnp.stack(c_s[4]), jnp.stack(c_s[5]),
            jnp.stack(c_p[0]), jnp.stack(c_p[1]), jnp.stack(c_p[2]), jnp.stack(c_p[3]),
            jnp.stack(c_s[0]), jnp.stack(c_s[1]), jnp.stack(c_s[2]), jnp.stack(c_s[3]))
```

```python
import functools
import math

import numpy as np
import jax
import jax.numpy as jnp
from jax import lax
from jax.experimental import pallas as pl
from jax.experimental.pallas import tpu as pltpu

F32 = jnp.float32
BF16 = jnp.bfloat16
HIGHEST = lax.Precision.HIGHEST

N_MIXERS = 3
NORM_EPS = 1e-6
LN_EPS = 1e-5
L2_EPS = 1e-6
DN_DK = 128
DN_DV = 128
DN_HK = 8
DN_HV = 16
DN_CONV = 4
DN_CHUNK = 64
INV_BLOCK = 16
DN_QK = DN_HK * DN_DK
DN_VW = DN_HV * DN_DV
DN_CONV_DIM = 2 * DN_QK + DN_VW
DN_HEADS_ALL = DN_CONV_DIM // DN_DK
SG_CHUNK = 128
SG_GROUPS = 8
NSA_DH = 64
NSA_HQ = 16
NSA_HKV = 4
NSA_G = NSA_HQ // NSA_HKV
CMP_STRIDE = 16
CMP_BLOCK = 32
SLC_BLOCK = 64
N_SEL = 16
WINDOW = 512
Q_BLOCK = 128
FORCE_BONUS = 1e4
ROPE_THETA = 10000.0
NSA_QW = NSA_HQ * NSA_DH
NSA_KW = NSA_HKV * NSA_DH
SEG_W = CMP_STRIDE * NSA_KW
NEG = -1e30
QK_SCALE_LOG2 = NSA_DH ** -0.5 * math.log2(math.e)

LANE = 128
SUBLANE = 8
VMEM_LIMIT = 56 * 1024 * 1024


def _cparams(sem):
    return pltpu.CompilerParams(dimension_semantics=sem, vmem_limit_bytes=VMEM_LIMIT)


def _round_up(n, m):
    return -(-n // m) * m


def _pick(n, cands):
    for c in cands:
        if n % c == 0:
            return c
    raise ValueError(f"no tile for {n} in {cands}")


def _dot(a, b):
    return jnp.dot(a, b, preferred_element_type=F32)


def _dot_nt(a, b, precision=None):
    return lax.dot_general(a, b, (((1,), (1,)), ((), ())), precision=precision, preferred_element_type=F32)


def _dotf(a, b):
    return jnp.dot(a, b, precision=HIGHEST, preferred_element_type=F32)


def _split_bf16(a):
    hi = a.astype(BF16)
    return hi, (a - hi.astype(F32)).astype(BF16)


def _dot3_nt(a, b):
    ah, al = _split_bf16(a)
    bh, bl = _split_bf16(b)
    return _dot_nt(ah, bh) + (_dot_nt(ah, bl) + _dot_nt(al, bh))


def _rms(x, g, eps=NORM_EPS):
    return x * lax.rsqrt(jnp.mean(x * x, -1, keepdims=True) + eps) * g


def _resident(shape):
    nd = len(shape)
    return pl.BlockSpec(shape, lambda *_: (0,) * nd, pipeline_mode=pl.Buffered(1))


def _rms_matmul_body(x_ref, g_ref, w_ref, o_ref, *, cn, act):
    xb = _rms(x_ref[...], g_ref[...]).astype(BF16)
    for c0 in range(0, o_ref.shape[-1], cn):
        y = _dot(xb, w_ref[:, c0:c0 + cn])
        if act == "gelu":
            y = jax.nn.gelu(y)
        o_ref[:, c0:c0 + cn] = y


def rms_matmul(x, g, w, *, act=None):
    T, D = x.shape
    N = w.shape[1]
    tm = _pick(T, (256, 128, 64, 32, 16, 8))
    k = 1
    while not (N % k == 0 and (N // k) % LANE == 0 and N // k <= 1024):
        k += 1
    return pl.pallas_call(
        functools.partial(_rms_matmul_body, cn=N // k, act=act),
        out_shape=jax.ShapeDtypeStruct((T, N), F32),
        grid=(T // tm,),
        in_specs=[pl.BlockSpec((tm, D), lambda i: (i, 0)), _resident((1, D)), _resident((D, N))],
        out_specs=pl.BlockSpec((tm, N), lambda i: (i, 0)),
        compiler_params=_cparams(("parallel",)),
        name="rms_matmul",
    )(x, g.reshape(1, D), w)


def _ffn_body(x_ref, g_ref, wg_ref, wu_ref, wd_ref, o_ref, *, hc):
    x = x_ref[...]
    xb = _rms(x, g_ref[...]).astype(BF16)
    acc = x
    for c0 in range(0, wg_ref.shape[1], hc):
        gate = _dot(xb, wg_ref[:, c0:c0 + hc])
        up = _dot(xb, wu_ref[:, c0:c0 + hc])
        a = (jax.nn.silu(gate) * up).astype(BF16)
        acc = acc + _dot(a, wd_ref[c0:c0 + hc, :])
    o_ref[...] = acc


def ffn(x, g, wg, wu, wd):
    T, D = x.shape
    H = wg.shape[1]
    tm = _pick(T, (512, 256, 128, 64, 32, 16, 8))
    hc = H // 2 if (H // 2) % LANE == 0 else H
    return pl.pallas_call(
        functools.partial(_ffn_body, hc=hc),
        out_shape=jax.ShapeDtypeStruct((T, D), F32),
        grid=(T // tm,),
        in_specs=[pl.BlockSpec((tm, D), lambda i: (i, 0)), _resident((1, D)),
                  _resident((D, H)), _resident((D, H)), _resident((H, D))],
        out_specs=pl.BlockSpec((tm, D), lambda i: (i, 0)),
        compiler_params=_cparams(("parallel",)),
        name="ffn",
    )(x, g.reshape(1, D), wg, wu, wd)


def _mm_res_body(a_ref, w_ref, r_ref, o_ref):
    o_ref[...] = r_ref[...] + _dot(a_ref[...].astype(BF16), w_ref[...])


def mm_res(a, w, res):
    T, K = a.shape
    D = w.shape[1]
    tm = _pick(T, (512, 256, 128, 64, 32, 16, 8))
    return pl.pallas_call(
        _mm_res_body,
        out_shape=jax.ShapeDtypeStruct((T, D), F32),
        grid=(T // tm,),
        in_specs=[pl.BlockSpec((tm, K), lambda i: (i, 0)), _resident((K, D)),
                  pl.BlockSpec((tm, D), lambda i: (i, 0))],
        out_specs=pl.BlockSpec((tm, D), lambda i: (i, 0)),
        compiler_params=_cparams(("parallel",)),
        name="mm_res",
    )(a, w, res)


def _dn_gates_body(p_ref, alog_ref, dtb_ref, beta_ref, g_ref):
    p = p_ref[...]
    beta_ref[...] = jax.nn.sigmoid(p[:, :DN_HV])
    g_ref[...] = -jnp.exp(alog_ref[...]) * jax.nn.softplus(p[:, DN_HV:2 * DN_HV] + dtb_ref[...])


def dn_gates(proj, a_log, dt_bias):
    T = proj.shape[0]
    tm = _pick(T, (512, 256, 128, 64, 32, 16, 8))
    col_blk = (DN_CONV_DIM + DN_VW) // LANE
    return pl.pallas_call(
        _dn_gates_body,
        out_shape=(jax.ShapeDtypeStruct((T, DN_HV), F32),) * 2,
        grid=(T // tm,),
        in_specs=[pl.BlockSpec((tm, LANE), lambda i: (i, col_blk)), _resident((1, DN_HV)), _resident((1, DN_HV))],
        out_specs=(pl.BlockSpec((tm, DN_HV), lambda i: (i, 0)),) * 2,
        compiler_params=_cparams(("parallel",)),
        name="dn_gates",
    )(proj, a_log.reshape(1, DN_HV), dt_bias.reshape(1, DN_HV))


def _dn_conv_body(x_ref, prev_ref, first_ref, w_ref, o_ref, *, nt):
    seq_start = lax.rem(pl.program_id(0), nt) == 0
    for h in range(DN_HEADS_ALL):
        sl = slice(h * DN_DK, (h + 1) * DN_DK)
        x = x_ref[:, sl]
        xc = jnp.concatenate([jnp.where(seq_start, first_ref[0, :, sl], prev_ref[:, sl]), x], axis=0)
        w = w_ref[:, sl]
        acc = x * w[DN_CONV - 1:DN_CONV]
        for k in range(1, DN_CONV):
            shifted = pltpu.roll(xc, k, 0)[SUBLANE:]
            acc = acc + shifted * w[DN_CONV - 1 - k:DN_CONV - k]
        c = jax.nn.silu(acc)
        if h < 2 * DN_HK:
            c = c * lax.rsqrt(jnp.sum(c * c, -1, keepdims=True) + L2_EPS)
        if h < DN_HK:
            c = c * DN_DK ** -0.5
        o_ref[h] = c


def dn_conv(x, first, conv_w, rows, tt, nt):
    per8 = tt // SUBLANE
    return pl.pallas_call(
        functools.partial(_dn_conv_body, nt=nt),
        out_shape=jax.ShapeDtypeStruct((DN_HEADS_ALL, rows, DN_DK), F32),
        grid=(rows // tt,),
        in_specs=[pl.BlockSpec((tt, DN_CONV_DIM), lambda i: (i, 0)),
                  pl.BlockSpec((SUBLANE, DN_CONV_DIM), lambda i: (jnp.maximum(i * per8 - 1, 0), 0)),
                  pl.BlockSpec((1, SUBLANE, DN_CONV_DIM), lambda i: (i // nt, 0, 0)),
                  _resident((DN_CONV, DN_CONV_DIM))],
        out_specs=pl.BlockSpec((DN_HEADS_ALL, tt, DN_DK), lambda i: (0, i, 0)),
        compiler_params=_cparams(("parallel",)),
        name="dn_conv",
    )(x, x, first, conv_w)


def _bdot(a, b):
    return lax.dot_general(a, b, (((2,), (1,)), ((0,), (0,))), preferred_element_type=F32)


def _bdot_nt(a, b):
    return lax.dot_general(a, b, (((2,), (2,)), ((0,), (0,))), preferred_element_type=F32)


def _bdot3(a, b):
    ah, al = _split_bf16(a)
    bh, bl = _split_bf16(b)
    return _bdot(ah, bh) + (_bdot(ah, bl) + _bdot(al, bh))


def _delta_body(q_ref, k_ref, v_ref, g_ref, b_ref, s0_ref, o_ref, sout_ref, S_ref, *, hb):
    c = pl.program_id(1)
    C = DN_CHUNK

    @pl.when(c == 0)
    def _():
        S_ref[...] = s0_ref[0]

    ri = lax.broadcasted_iota(jnp.int32, (C, C), 0)
    ci = lax.broadcasted_iota(jnp.int32, (C, C), 1)
    tri = ri >= ci
    stri = ri > ci
    eye = jnp.where(ri == ci, 1.0, 0.0).astype(F32)
    bi, bj = ri // INV_BLOCK, ci // INV_BLOCK
    rep = DN_HV // DN_HK
    nq = hb // rep
    q = q_ref[...][:, None]
    k = k_ref[...][:, None]
    kb16 = k_ref[...].astype(BF16)
    kk = _bdot_nt(kb16, kb16)[:, None]
    qk0 = _bdot_nt(q_ref[...].astype(BF16), kb16)[:, None]
    grow = g_ref[...].reshape(nq, rep, 1, C)
    brow = b_ref[...].reshape(nq, rep, 1, C)
    gcol = jnp.sum(eye * grow, -1, keepdims=True)
    bcol = jnp.sum(eye * brow, -1, keepdims=True)
    gc_col = jnp.sum(jnp.where(tri, grow, 0.0), -1, keepdims=True)
    gc_row = jnp.sum(jnp.where(ri <= ci, gcol, 0.0), -2, keepdims=True)
    decay = jnp.exp(jnp.where(tri, gc_col - gc_row, NEG))
    a_mat = (jnp.where(stri, kk * decay, 0.0) * bcol).reshape(hb, C, C)
    eg = jnp.exp(gc_col)
    v = v_ref[...].reshape(nq, rep, C, DN_DV)
    rhs = jnp.concatenate([v * bcol, k * (bcol * eg)], axis=-1).reshape(hb, C, 2 * DN_DV)
    p = -jnp.where(bi == bj, a_mat, 0.0)
    t = eye + p
    for _ in range(int(math.log2(INV_BLOCK)) - 1):
        p = _bdot3(p, p)
        t = t + _bdot3(t, p)
    width = INV_BLOCK
    while width < C:
        e = jnp.where((bi // (2 * width // INV_BLOCK) == bj // (2 * width // INV_BLOCK))
                      & (bi // (width // INV_BLOCK) != bj // (width // INV_BLOCK)), a_mat, 0.0)
        t = t - _bdot3(_bdot3(t, e), t)
        width *= 2
    sol = _bdot3(t, rhs)
    g_last = gc_col[:, :, C - 1:C]
    u = sol[:, :, :DN_DV]
    wq = jnp.concatenate([sol[:, :, DN_DV:], (q * eg).reshape(hb, C, DN_DK)], axis=1).astype(BF16)
    qk = jnp.where(tri, qk0 * decay, 0.0).reshape(hb, C, C).astype(BF16)
    kdec_t = jnp.swapaxes((k * jnp.exp(g_last - gc_col)).reshape(hb, C, DN_DK), 1, 2).astype(BF16)
    g_keep = jnp.exp(g_last).reshape(hb, 1, 1)

    S = S_ref[...]
    ws = _bdot(wq, S.astype(BF16))
    vb = (u - ws[:, :C]).astype(BF16)
    o_ref[...] = ws[:, C:] + _bdot(qk, vb)
    S_ref[...] = S * g_keep + _bdot(kdec_t, vb)

    @pl.when(c == pl.num_programs(1) - 1)
    def _():
        sout_ref[0] = S_ref[...]


def delta_rule(qkv_hm, g_rows, b_rows, s0, B, n):
    hb = DN_HV
    C = DN_CHUNK
    state_blk = pl.BlockSpec((1, hb, DN_DK, DN_DV), lambda b, c: (b, 0, 0, 0))
    gate_blk = pl.BlockSpec((hb, 1, 1, C), lambda b, c: (0, b * n + c, 0, 0))
    return pl.pallas_call(
        functools.partial(_delta_body, hb=hb),
        out_shape=(jax.ShapeDtypeStruct((DN_HV, B * n * C, DN_DV), F32),
                   jax.ShapeDtypeStruct((B, DN_HV, DN_DK, DN_DV), F32)),
        grid=(B, n),
        in_specs=[
            pl.BlockSpec((DN_HK, C, DN_DK), lambda b, c: (0, b * n + c, 0)),
            pl.BlockSpec((DN_HK, C, DN_DK), lambda b, c: (1, b * n + c, 0)),
            pl.BlockSpec((DN_HV, C, DN_DV), lambda b, c: (1, b * n + c, 0)),
            gate_blk, gate_blk, state_blk,
        ],
        out_specs=(pl.BlockSpec((hb, C, DN_DV), lambda b, c: (0, b * n + c, 0)), state_blk),
        scratch_shapes=[pltpu.VMEM((hb, DN_DK, DN_DV), F32)],
        compiler_params=_cparams(("parallel", "arbitrary")),
        name="delta_rule",
    )(qkv_hm, qkv_hm, qkv_hm, g_rows, b_rows, s0)


def _dn_out_body(o_ref, z_ref, g_ref, w_ref, r_ref, out_ref):
    parts = []
    for h in range(DN_HV):
        z = z_ref[:, h * DN_DV:(h + 1) * DN_DV]
        parts.append((_rms(o_ref[h], g_ref[...]) * jax.nn.silu(z)).astype(BF16))
    out_ref[...] = r_ref[...] + _dot(jnp.concatenate(parts, axis=-1), w_ref[...])


def dn_out(o_hm, proj, norm_g, w_out, res):
    T, D = res.shape
    tm = _pick(T, (256, 128, 64, 32, 16, 8))
    return pl.pallas_call(
        _dn_out_body,
        out_shape=jax.ShapeDtypeStruct((T, D), F32),
        grid=(T // tm,),
        in_specs=[pl.BlockSpec((DN_HV, tm, DN_DV), lambda i: (0, i, 0)),
                  pl.BlockSpec((tm, DN_VW), lambda i: (i, DN_CONV_DIM // DN_VW)),
                  _resident((1, DN_DV)), _resident((DN_VW, D)),
                  pl.BlockSpec((tm, D), lambda i: (i, 0))],
        out_specs=pl.BlockSpec((tm, D), lambda i: (i, 0)),
        compiler_params=_cparams(("parallel",)),
        name="dn_out",
    )(o_hm, proj, norm_g.reshape(1, DN_DV), w_out, res)


def _delta_group(h, proj, B, L, conv_buf, s0, conv_w, a_log, dt_bias, norm_g, w_out):
    C = DN_CHUNK
    Lp = _round_up(L, C)
    n = Lp // C
    tail = proj.reshape(B, L, proj.shape[1])[:, max(L - (DN_CONV - 1), 0):, :DN_CONV_DIM]
    new_buf = jnp.concatenate([conv_buf, tail], axis=1)[:, -(DN_CONV - 1):]
    tt = _pick(Lp, (256, 128, 64))
    nt = Lp // tt
    if Lp == L:
        x = proj
    else:
        qkv = proj[:, :DN_CONV_DIM].reshape(B, L, DN_CONV_DIM)
        x = jnp.pad(qkv, ((0, 0), (0, Lp - L), (0, 0))).reshape(B * Lp, DN_CONV_DIM)
    first = jnp.pad(conv_buf, ((0, 0), (SUBLANE - (DN_CONV - 1), 0), (0, 0)))
    qkv_hm = dn_conv(x, first, conv_w, B * Lp, tt, nt)
    beta, g = dn_gates(proj, a_log, dt_bias)

    def rows(a):
        a = jnp.pad(a.reshape(B, L, DN_HV), ((0, 0), (0, Lp - L), (0, 0)))
        return a.reshape(B, n, C, DN_HV).transpose(3, 0, 1, 2).reshape(DN_HV, B * n, 1, C)

    o_hm, S = delta_rule(qkv_hm, rows(g), rows(beta), s0, B, n)
    if Lp != L:
        o_hm = o_hm.reshape(DN_HV, B, Lp, DN_DV)[:, :, :L].reshape(DN_HV, B * L, DN_DV)
    return dn_out(o_hm, proj, norm_g, w_out, h), S, new_buf


def _sg_body(y_ref, lng_ref, lnb_ref, wsp_ref, bsp_ref, a_ref, *v_out, width):
    u = y_ref[:, :width]
    v = y_ref[:, width:]
    xc = v - jnp.mean(v, -1, keepdims=True)
    vn = xc * lax.rsqrt(jnp.mean(xc * xc, -1, keepdims=True) + LN_EPS) * lng_ref[...] + lnb_ref[...]
    if v_out:
        v_out[0][...] = vn
    n = wsp_ref.shape[1]
    gw = width // SG_GROUPS
    tril = lax.broadcasted_iota(jnp.int32, (n, n), 0) >= lax.broadcasted_iota(jnp.int32, (n, n), 1)
    for g in range(SG_GROUPS):
        sl = slice(g * gw, (g + 1) * gw)
        wt = jnp.where(tril, wsp_ref[g], 0.0).astype(BF16)
        mixed = _dot(wt, vn[:, sl].astype(BF16)) + bsp_ref[:, g:g + 1]
        a_ref[:, sl] = u[:, sl] * mixed


def sg_mix(y, ln_g, ln_b, w_sp, b_sp, want_v):
    T = y.shape[0]
    W = y.shape[1] // 2
    n = SG_CHUNK
    out_shape = [jax.ShapeDtypeStruct((T, W), F32)]
    out_specs = [pl.BlockSpec((n, W), lambda i: (i, 0))]
    if want_v:
        out_shape.append(jax.ShapeDtypeStruct((T, W), F32))
        out_specs.append(pl.BlockSpec((n, W), lambda i: (i, 0)))
    return pl.pallas_call(
        functools.partial(_sg_body, width=W),
        out_shape=tuple(out_shape),
        grid=(T // n,),
        in_specs=[pl.BlockSpec((n, 2 * W), lambda i: (i, 0)), _resident((1, W)), _resident((1, W)),
                  _resident((SG_GROUPS, n, n)), _resident((n, SG_GROUPS))],
        out_specs=tuple(out_specs),
        compiler_params=_cparams(("parallel",)),
        name="sg_mix",
    )(y, ln_g.reshape(1, W), ln_b.reshape(1, W), w_sp, b_sp.T)


def _nsa_heads_body(p_ref, cos_ref, sin_ref, qg_ref, kg_ref, qn_ref, qr_ref, ksh_ref, vsh_ref, kwh_ref, vwh_ref,
                    kc_ref, vc_ref, kst_ref, vst_ref, kwt_ref, vwt_ref):
    cos, sin = cos_ref[...], sin_ref[...]
    half = NSA_DH // 2

    def head(col0, h):
        return p_ref[:, col0 + h * NSA_DH:col0 + (h + 1) * NSA_DH]

    def rope(xn):
        return xn * cos + jnp.concatenate([xn[:, half:], xn[:, :half]], axis=-1) * sin

    for h in range(NSA_HQ):
        xn = _rms(head(0, h), qg_ref[...])
        qn_ref[h] = xn
        qr_ref[h] = rope(xn)
    col = NSA_QW
    kc_ref[...] = p_ref[:, col:col + NSA_KW]
    vc_ref[...] = p_ref[:, col + NSA_KW:col + 2 * NSA_KW]
    col += 2 * NSA_KW
    for kind, (kh_ref, kt_ref, vh_ref, vt_ref) in enumerate(((ksh_ref, kst_ref, vsh_ref, vst_ref),
                                                             (kwh_ref, kwt_ref, vwh_ref, vwt_ref))):
        parts = []
        for h in range(NSA_HKV):
            r = rope(_rms(head(col, h), kg_ref[kind + 1:kind + 2]))
            kh_ref[h] = r.astype(BF16)
            parts.append(r)
            vh_ref[h] = head(col + NSA_KW, h).astype(BF16)
        kt_ref[...] = jnp.concatenate(parts, axis=-1)
        vt_ref[...] = p_ref[:, col + NSA_KW:col + 2 * NSA_KW]
        col += 2 * NSA_KW


def nsa_heads(proj, cos, sin, q_norm, k_norm):
    T, N = proj.shape
    Tc = cos.shape[0]
    tl = _pick(Tc, (256, 128, 64, 32, 16, 8))
    n_tab = Tc // tl
    tab = pl.BlockSpec((tl, NSA_DH), lambda i: (i % n_tab, 0))
    hm = lambda H: pl.BlockSpec((H, tl, NSA_DH), lambda i: (0, i, 0))
    tok = pl.BlockSpec((tl, NSA_KW), lambda i: (i, 0))
    return pl.pallas_call(
        _nsa_heads_body,
        out_shape=((jax.ShapeDtypeStruct((NSA_HQ, T, NSA_DH), F32),) * 2
                   + (jax.ShapeDtypeStruct((NSA_HKV, T, NSA_DH), BF16),) * 4
                   + (jax.ShapeDtypeStruct((T, NSA_KW), F32),) * 6),
        grid=(T // tl,),
        in_specs=[pl.BlockSpec((tl, N), lambda i: (i, 0)), tab, tab, _resident((1, NSA_DH)), _resident((3, NSA_DH))],
        out_specs=(hm(NSA_HQ),) * 2 + (hm(NSA_HKV),) * 4 + (tok,) * 6,
        compiler_params=_cparams(("parallel",)),
        name="nsa_heads",
    )(proj, cos, sin, q_norm.reshape(1, NSA_DH), k_norm)


def _cmp_proj_body(x_ref, w_ref, o_ref):
    o_ref[...] = _dot(x_ref[...].astype(BF16), w_ref[...])


def cmp_proj(x, wbig):
    rows = x.shape[0]
    tm = _pick(rows, (512, 256, 128, 64, 32, 16, 8))
    return pl.pallas_call(
        _cmp_proj_body,
        out_shape=jax.ShapeDtypeStruct((rows, 2 * NSA_KW), F32),
        grid=(rows // tm,),
        in_specs=[pl.BlockSpec((tm, SEG_W), lambda i: (i, 0)), _resident((SEG_W, 2 * NSA_KW))],
        out_specs=pl.BlockSpec((tm, 2 * NSA_KW), lambda i: (i, 0)),
        compiler_params=_cparams(("parallel",)),
        name="cmp_proj",
    )(x, wbig)


def _page_copy(pool_ref, buf_ref, sem_ref, page, slot, p):
    return pltpu.make_async_copy(pool_ref.at[page], buf_ref.at[slot, p], sem_ref.at[slot])


def _fetch_pages(pt_ref, pool_ref, buf_ref, sem_ref, b, chunk, slot, pch):
    for p in range(pch):
        _page_copy(pool_ref, buf_ref, sem_ref, pt_ref[b, chunk * pch + p], slot, p).start()


def _wait_pages(pool_ref, buf_ref, sem_ref, slot, pch):
    for p in range(pch):
        _page_copy(pool_ref, buf_ref, sem_ref, 0, slot, p).wait()


def _cmp_proj_paged_body(pt_ref, pool_ref, w_ref, o_ref, buf_ref, sem_ref, *, pch):
    b, ch = pl.program_id(0), pl.program_id(1)
    nb, nch = pl.num_programs(0), pl.num_programs(1)
    step = b * nch + ch
    slot = lax.rem(step, 2)

    @pl.when(step == 0)
    def _():
        _fetch_pages(pt_ref, pool_ref, buf_ref, sem_ref, b, ch, slot, pch)

    @pl.when(step + 1 < nb * nch)
    def _():
        nxt = step + 1
        _fetch_pages(pt_ref, pool_ref, buf_ref, sem_ref, nxt // nch, lax.rem(nxt, nch), 1 - slot, pch)

    _wait_pages(pool_ref, buf_ref, sem_ref, slot, pch)
    x = buf_ref[slot].reshape(pch * buf_ref.shape[2], SEG_W)
    o_ref[...] = _dot(x.astype(BF16), w_ref[...])


def cmp_proj_paged(page_table, pool, wbig):
    B, n_pages = page_table.shape
    segs = pool.shape[1]
    pch = _pick(n_pages, (32, 16, 8, 4, 2, 1))
    nch = n_pages // pch
    return pl.pallas_call(
        functools.partial(_cmp_proj_paged_body, pch=pch),
        out_shape=jax.ShapeDtypeStruct((B * n_pages * segs, 2 * NSA_KW), F32),
        grid_spec=pltpu.PrefetchScalarGridSpec(
            num_scalar_prefetch=1,
            grid=(B, nch),
            in_specs=[pl.BlockSpec(memory_space=pl.ANY),
                      pl.BlockSpec((SEG_W, 2 * NSA_KW), lambda b, c, pt: (0, 0))],
            out_specs=pl.BlockSpec((pch * segs, 2 * NSA_KW), lambda b, c, pt: (b * nch + c, 0)),
            scratch_shapes=[pltpu.VMEM((2, pch, segs, SEG_W), F32), pltpu.SemaphoreType.DMA((2,))],
        ),
        compiler_params=_cparams(("arbitrary", "arbitrary")),
        name="cmp_proj_paged",
    )(page_table, pool, wbig)


def _cmp_finish_body(a_ref, pe_ref, w_ref, g_ref, o_ref, *, do_norm):
    a = a_ref[0]
    n = a.shape[0]
    pe = _dot(pe_ref[...].astype(BF16), w_ref[...])
    pe_term = pe[0:1, :NSA_KW] + pe[1:2, NSA_KW:]
    blk = a[:, :NSA_KW] + pltpu.roll(a[:, NSA_KW:], n - 1, 0) + pe_term
    for h in range(NSA_HKV):
        bh = blk[:, h * NSA_DH:(h + 1) * NSA_DH]
        if do_norm:
            bh = _rms(bh, g_ref[...])
        o_ref[0, h] = bh


def cmp_finish(a, pe_rows, wbig, g, B, n, do_norm):
    return pl.pallas_call(
        functools.partial(_cmp_finish_body, do_norm=do_norm),
        out_shape=jax.ShapeDtypeStruct((B, NSA_HKV, n, NSA_DH), F32),
        grid=(B,),
        in_specs=[pl.BlockSpec((1, n, 2 * NSA_KW), lambda b: (b, 0, 0)), _resident((SUBLANE, SEG_W)),
                  _resident((SEG_W, 2 * NSA_KW)), _resident((1, NSA_DH))],
        out_specs=pl.BlockSpec((1, NSA_HKV, n, NSA_DH), lambda b: (b, 0, 0, 0)),
        compiler_params=_cparams(("parallel",)),
        name="cmp_finish",
    )(a.reshape(B, n, 2 * NSA_KW), pe_rows, wbig, g.reshape(1, NSA_DH))


def _masked_softmax(s, mask):
    sm = jnp.where(mask, s, NEG)
    m = jnp.max(sm, -1, keepdims=True)
    p = jnp.where(mask, jnp.exp(s - m), 0.0)
    d = jnp.sum(p, -1, keepdims=True)
    return p / jnp.where(d > 0, d, 1.0)


def _topk_mask(score, k):
    nl = score.shape[-1]
    lane = lax.broadcasted_iota(jnp.int32, score.shape, 1).astype(F32)
    sel = jnp.zeros(score.shape, F32)
    work = score
    for _ in range(k):
        m = jnp.max(work, -1, keepdims=True)
        idx = jnp.min(jnp.where(work == m, lane, float(nl)), -1, keepdims=True)
        hit = lane == idx
        sel = jnp.where(hit, 1.0, sel)
        work = jnp.where(hit, -jnp.inf, work)
    return sel


def _select_blocks(imp, qpos, n_lanes_valid):
    blk = lax.broadcasted_iota(jnp.int32, imp.shape, 1)
    cur = qpos // SLC_BLOCK
    eligible = (blk * SLC_BLOCK <= qpos) & (blk < n_lanes_valid)
    forced = (blk == 0) | (blk == cur) | (blk == cur - 1)
    score = jnp.where(eligible, imp + jnp.where(forced, FORCE_BONUS, 0.0), -jnp.inf)
    return _topk_mask(score, N_SEL)


def _flash_tile(q3, k, v, mask, carry):
    return _flash_tile_bias(q3, k, v, jnp.where(mask, 0.0, NEG), carry)


def _flash_tile_bias(q3, k, v, bias, carry):
    m, l, acc = carry
    G, R, dh = q3.shape
    tk = k.shape[0]
    s = _dot_nt(q3.reshape(G * R, dh), k).reshape(G, R, tk) + bias[None]
    m_new = jnp.maximum(m, jnp.max(s, -1, keepdims=True))
    alpha = jnp.exp2(m - m_new)
    p = jnp.exp2(s - m_new)
    l = alpha * l + jnp.sum(p, -1, keepdims=True)
    pv = _dot(p.reshape(G * R, tk).astype(BF16), v).reshape(G, R, dh)
    return m_new, l, alpha * acc + pv


def _flash_init(G, R, dh):
    return (jnp.full((G, R, 1), NEG, F32), jnp.zeros((G, R, 1), F32), jnp.zeros((G, R, dh), F32))


def _flash_done(carry):
    m, l, acc = carry
    return jnp.where(m > 0.5 * NEG, acc / l, 0.0)


def _nsa_attn_body(qn_ref, q_ref, kc_ref, vc_ref, m_ref, ks_ref, vs_ref, kw_ref, vw_ref, gt_ref, o_ref, *, tk, n_slc):
    i = pl.program_id(2)
    qb = q_ref.shape[1]
    nb = m_ref.shape[1]
    rows = NSA_G * qb
    qpos = i * qb + lax.broadcasted_iota(jnp.int32, (qb, 1), 0)

    kc = kc_ref[0, 0]
    ncp = kc.shape[0]
    s = _dot3_nt(qn_ref[...].reshape(rows, NSA_DH) * NSA_DH ** -0.5, kc).reshape(NSA_G, qb, ncp)
    kend = lax.broadcasted_iota(jnp.int32, (1, ncp), 1) * CMP_STRIDE + (CMP_BLOCK - 1)
    p = _masked_softmax(s, (kend <= qpos)[None])
    o_cmp = _dot(p.reshape(rows, ncp).astype(BF16), vc_ref[0, 0].astype(BF16)).reshape(NSA_G, qb, NSA_DH)
    psum = p[0]
    for g in range(1, NSA_G):
        psum = psum + p[g]
    ps_hi, ps_lo = _split_bf16(psum)
    overlap = m_ref[...].astype(BF16)
    sel = _select_blocks(_dot(ps_hi, overlap) + _dot(ps_lo, overlap), qpos, n_slc)
    sel_bias = jnp.where(sel > 0.5, 0.0, NEG)

    q3 = (q_ref[...] * QK_SCALE_LOG2).astype(BF16)
    slab = WINDOW + qb
    start = pl.multiple_of(jnp.maximum(i * qb - WINDOW, 0), qb)
    d = qpos - (start + lax.broadcasted_iota(jnp.int32, (1, slab), 1))
    o_swa = _flash_done(_flash_tile(q3, kw_ref[0, pl.ds(start, slab), :], vw_ref[0, pl.ds(start, slab), :],
                                    (d >= 0) & (d <= WINDOW), _flash_init(NSA_G, qb, NSA_DH)))

    bpt = tk // SLC_BLOCK
    expand = jnp.where(lax.broadcasted_iota(jnp.int32, (nb, tk), 0)
                       == lax.broadcasted_iota(jnp.int32, (nb, tk), 1) // SLC_BLOCK, 1.0, 0.0).astype(BF16)

    def key_bias(j):
        rolled = pltpu.roll(sel_bias, lax.rem(nb - j * bpt, nb), 1)
        return _dot(rolled.astype(BF16), expand)

    def slc_step(j, carry):
        start = pl.multiple_of(j * tk, tk)
        return _flash_tile_bias(q3, ks_ref[0, pl.ds(start, tk), :], vs_ref[0, pl.ds(start, tk), :],
                                key_bias(j), carry)

    assert tk % qb == 0
    n_full = (i * qb) // tk
    carry = lax.fori_loop(0, n_full, slc_step, _flash_init(NSA_G, qb, NSA_DH))
    start = pl.multiple_of(n_full * tk, tk)
    kpos = n_full * tk + lax.broadcasted_iota(jnp.int32, (1, tk), 1)
    diag_bias = key_bias(n_full) + jnp.where(kpos <= qpos, 0.0, NEG)
    o_slc = _flash_done(_flash_tile_bias(q3, ks_ref[0, pl.ds(start, tk), :], vs_ref[0, pl.ds(start, tk), :],
                                         diag_bias, carry))

    gt = jax.nn.sigmoid(gt_ref[0, 0])
    parts = []
    for g in range(NSA_G):
        parts.append(gt[:, 3 * g:3 * g + 1] * o_cmp[g] + gt[:, 3 * g + 1:3 * g + 2] * o_slc[g]
                     + gt[:, 3 * g + 2:3 * g + 3] * o_swa[g])
    o_ref[0] = jnp.concatenate(parts, axis=-1)


def nsa_attn(qn, qr, kcmp, vcmp, m_mat, ks, vs, kw, vw, gates, n_slc):
    B, _, ncp, dh = kcmp.shape
    L = qr.shape[1] // B
    nb = m_mat.shape[1]
    qb = Q_BLOCK
    tk = _pick(L, (1024, 512, 256, 128))
    assert L >= WINDOW + qb
    q_spec = pl.BlockSpec((NSA_G, qb, dh), lambda b, h, i: (h, b * (L // qb) + i, 0))
    cmp_spec = pl.BlockSpec((1, 1, ncp, dh), lambda b, h, i: (b, h, 0, 0))
    kv_spec = pl.BlockSpec((1, L, dh), lambda b, h, i: (h, b, 0))
    return pl.pallas_call(
        functools.partial(_nsa_attn_body, tk=tk, n_slc=n_slc),
        out_shape=jax.ShapeDtypeStruct((B, L, NSA_QW), F32),
        grid=(B, NSA_HKV, L // qb),
        in_specs=[q_spec, q_spec, cmp_spec, cmp_spec, _resident((ncp, nb)), kv_spec, kv_spec, kv_spec, kv_spec,
                  pl.BlockSpec((1, 1, qb, 3 * NSA_G), lambda b, h, i: (b, h, i, 0))],
        out_specs=pl.BlockSpec((1, qb, NSA_G * dh), lambda b, h, i: (b, i, h)),
        compiler_params=_cparams(("parallel", "parallel", "parallel")),
        name="nsa_attn",
    )(qn, qr, kcmp, vcmp, m_mat, ks, vs, kw, vw, gates)


def _sample_attn_body(pt_ref, qn_ref, qr_ref, kc_ref, vc_ref, kpool_ref, vpool_ref, ksn_ref, vsn_ref,
                      bk_ref, bv_ref, kwn_ref, vwn_ref, gt_ref, m_ref, o_ref,
                      kbuf, vbuf, ksem, vsem, sel_s, m_s, l_s, acc_s, *, pch, L, pos0, n_slc):
    b, ch = pl.program_id(0), pl.program_id(1)
    nb, nch = pl.num_programs(0), pl.num_programs(1)
    step = b * nch + ch
    slot = lax.rem(step, 2)
    R = NSA_G * L
    page = kbuf.shape[2]
    tk = pch * page

    @pl.when(step == 0)
    def _():
        _fetch_pages(pt_ref, kpool_ref, kbuf, ksem, b, ch, slot, pch)
        _fetch_pages(pt_ref, vpool_ref, vbuf, vsem, b, ch, slot, pch)

    @pl.when(step + 1 < nb * nch)
    def _():
        nxt = step + 1
        _fetch_pages(pt_ref, kpool_ref, kbuf, ksem, nxt // nch, lax.rem(nxt, nch), 1 - slot, pch)
        _fetch_pages(pt_ref, vpool_ref, vbuf, vsem, nxt // nch, lax.rem(nxt, nch), 1 - slot, pch)

    row = lax.broadcasted_iota(jnp.int32, (R, 1), 0)
    tq = row & (L - 1)
    qpos = pos0 + tq
    same_t = jnp.where((lax.broadcasted_iota(jnp.int32, (R, R), 0) & (L - 1))
                       == (lax.broadcasted_iota(jnp.int32, (R, R), 1) & (L - 1)), 1.0, 0.0)
    scale = NSA_DH ** -0.5

    @pl.when(ch == 0)
    def _():
        for h in range(NSA_HKV):
            kc = kc_ref[0, h]
            ncp = kc.shape[0]
            s = _dot_nt(qn_ref[0, h] * scale, kc, HIGHEST)
            kend = lax.broadcasted_iota(jnp.int32, (1, ncp), 1) * CMP_STRIDE + (CMP_BLOCK - 1)
            p = _masked_softmax(s, kend <= qpos)
            acc_s[0, h] = _dot(p.astype(BF16), vc_ref[0, h].astype(BF16))
            imp = _dotf(_dotf(same_t, p), m_ref[...])
            sel_s[h] = _select_blocks(imp, qpos, n_slc)
            m_s[h] = jnp.full((R, 1), NEG, F32)
            l_s[h] = jnp.zeros((R, 1), F32)
            acc_s[1, h] = jnp.zeros((R, NSA_DH), F32)

    _wait_pages(kpool_ref, kbuf, ksem, slot, pch)
    _wait_pages(vpool_ref, vbuf, vsem, slot, pch)
    nbl = sel_s.shape[2]
    blk_i = lax.broadcasted_iota(jnp.int32, (nbl, tk), 0)
    key_blk = lax.broadcasted_iota(jnp.int32, (nbl, tk), 1) // SLC_BLOCK + ch * (tk // SLC_BLOCK)
    expand = jnp.where(blk_i == key_blk, 1.0, 0.0).astype(BF16)
    kall = kbuf[slot].reshape(tk, NSA_KW)
    vall = vbuf[slot].reshape(tk, NSA_KW)
    for h in range(NSA_HKV):
        sl = slice(h * NSA_DH, (h + 1) * NSA_DH)
        q3 = (qr_ref[0, h] * QK_SCALE_LOG2).astype(BF16)[None]
        mask = _dot(sel_s[h].astype(BF16), expand) > 0.5
        carry = (m_s[h][None], l_s[h][None], acc_s[1, h][None])
        m, l, acc = _flash_tile(q3, kall[:, sl].astype(BF16), vall[:, sl].astype(BF16), mask, carry)
        m_s[h], l_s[h], acc_s[1, h] = m[0], l[0], acc[0]

    @pl.when(ch == nch - 1)
    def _():
        npad = ksn_ref.shape[1]
        tn = lax.broadcasted_iota(jnp.int32, (1, npad), 1)
        new_ok = (tn <= tq) & (tn < L)
        wb = bk_ref.shape[1]
        wpos = pos0 - wb + lax.broadcasted_iota(jnp.int32, (1, wb), 1)
        dw = qpos - wpos
        win_ok = (dw >= 0) & (dw <= WINDOW)
        cur_blk = pos0 // SLC_BLOCK
        for h in range(NSA_HKV):
            sl = slice(h * NSA_DH, (h + 1) * NSA_DH)
            q3 = (qr_ref[0, h] * QK_SCALE_LOG2).astype(BF16)[None]
            sel_new = sel_s[h][:, cur_blk:cur_blk + 1] > 0.5
            carry = (m_s[h][None], l_s[h][None], acc_s[1, h][None])
            carry = _flash_tile(q3, ksn_ref[0][:, sl].astype(BF16), vsn_ref[0][:, sl].astype(BF16),
                                new_ok & sel_new, carry)
            o_slc = _flash_done(carry)[0]
            carry = _flash_init(1, R, NSA_DH)
            carry = _flash_tile(q3, bk_ref[0][:, sl].astype(BF16), bv_ref[0][:, sl].astype(BF16), win_ok, carry)
            carry = _flash_tile(q3, kwn_ref[0][:, sl].astype(BF16), vwn_ref[0][:, sl].astype(BF16), new_ok, carry)
            o_swa = _flash_done(carry)[0]
            gt = jax.nn.sigmoid(gt_ref[0, h])
            o_ref[0, h] = gt[:, 0:1] * acc_s[0, h] + gt[:, 1:2] * o_slc + gt[:, 2:3] * o_swa


def sample_attn(page_table, qn, qr, kcmp, vcmp, kpool, vpool, ks_new, vs_new, buf_k, buf_v, kw_new, vw_new,
                gates, m_mat, L, pos0, n_slc):
    B, n_pages = page_table.shape
    page = kpool.shape[1]
    R = NSA_G * L
    pch = _pick(n_pages, (16, 8, 4, 2, 1))
    nch = n_pages // pch
    nbl = m_mat.shape[1]
    ncp = kcmp.shape[2]
    assert L & (L - 1) == 0 and pos0 % SLC_BLOCK == 0 and pos0 // SLC_BLOCK < nbl
    per_b = lambda *blk: pl.BlockSpec((1,) + blk, lambda b, c, pt: (b,) + (0,) * len(blk))
    return pl.pallas_call(
        functools.partial(_sample_attn_body, pch=pch, L=L, pos0=pos0, n_slc=n_slc),
        out_shape=jax.ShapeDtypeStruct((B, NSA_HKV, R, NSA_DH), F32),
        grid_spec=pltpu.PrefetchScalarGridSpec(
            num_scalar_prefetch=1,
            grid=(B, nch),
            in_specs=[per_b(NSA_HKV, R, NSA_DH), per_b(NSA_HKV, R, NSA_DH),
                      per_b(NSA_HKV, ncp, NSA_DH), per_b(NSA_HKV, ncp, NSA_DH),
                      pl.BlockSpec(memory_space=pl.ANY), pl.BlockSpec(memory_space=pl.ANY),
                      per_b(SUBLANE, NSA_KW), per_b(SUBLANE, NSA_KW),
                      per_b(buf_k.shape[1], NSA_KW), per_b(buf_k.shape[1], NSA_KW),
                      per_b(SUBLANE, NSA_KW), per_b(SUBLANE, NSA_KW),
                      per_b(NSA_HKV, R, 3),
                      pl.BlockSpec((ncp, nbl), lambda b, c, pt: (0, 0))],
            out_specs=per_b(NSA_HKV, R, NSA_DH),
            scratch_shapes=[pltpu.VMEM((2, pch, page, NSA_KW), F32), pltpu.VMEM((2, pch, page, NSA_KW), F32),
                            pltpu.SemaphoreType.DMA((2,)), pltpu.SemaphoreType.DMA((2,)),
                            pltpu.VMEM((NSA_HKV, R, nbl), F32), pltpu.VMEM((NSA_HKV, R, 1), F32),
                            pltpu.VMEM((NSA_HKV, R, 1), F32), pltpu.VMEM((2, NSA_HKV, R, NSA_DH), F32)],
        ),
        compiler_params=_cparams(("arbitrary", "arbitrary")),
        name="sample_attn",
    )(page_table, qn, qr, kcmp, vcmp, kpool, vpool, ks_new, vs_new, buf_k, buf_v, kw_new, vw_new, gates, m_mat)


def _cmp_to_slc(n_cmp_rows, n_cols, n_slc):
    k = np.arange(n_cmp_rows)[:, None] * CMP_STRIDE
    j = np.arange(n_cols)[None, :] * SLC_BLOCK
    ov = np.minimum(k + CMP_BLOCK, j + SLC_BLOCK) - np.maximum(k, j)
    m = np.clip(ov, 0, None).astype(np.float32) / CMP_BLOCK
    m[:, n_slc:] = 0.0
    return jnp.asarray(m)


def _rope_tables(pos0, L):
    half = NSA_DH // 2
    inv = ROPE_THETA ** (-jnp.arange(half, dtype=F32) / half)
    ang = (pos0 + jnp.arange(L)).astype(F32)[:, None] * inv[None, :]
    cos, sin = jnp.cos(ang), jnp.sin(ang)
    return jnp.concatenate([cos, cos], -1), jnp.concatenate([-sin, sin], -1)


def _cmp_weights(w, pe):
    w = w.reshape(2, CMP_STRIDE, NSA_DH, NSA_DH)
    eye = jnp.eye(NSA_HKV, dtype=F32)
    wbig = jnp.einsum("alde,hg->lhdage", w, eye).reshape(SEG_W, 2 * NSA_KW).astype(BF16)
    pe = pe.reshape(2, CMP_STRIDE, 1, NSA_DH)
    pe_rows = jnp.broadcast_to(pe, (2, CMP_STRIDE, NSA_HKV, NSA_DH)).reshape(2, SEG_W)
    return wbig, jnp.pad(pe_rows, ((0, SUBLANE - 2), (0, 0)))


def _nsa_prompt(h, proj, B, L, q_norm, k_norm, pe_k, pe_v, w_ck, w_cv, w_out):
    cos, sin = _rope_tables(0, L)
    qn, qr, ks_hm, vs_hm, kw_hm, vw_hm, kc, vc, ks, vs, kw, vw = nsa_heads(proj, cos, sin, q_norm, k_norm)
    n_seg = L // CMP_STRIDE
    n_slc = L // SLC_BLOCK
    assert L % Q_BLOCK == 0 and n_slc <= LANE and n_slc >= N_SEL
    wk, pek = _cmp_weights(w_ck, pe_k)
    wv, pev = _cmp_weights(w_cv, pe_v)
    kcmp = cmp_finish(cmp_proj(kc.reshape(B * n_seg, SEG_W), wk), pek, wk, k_norm[0], B, n_seg, True)
    vcmp = cmp_finish(cmp_proj(vc.reshape(B * n_seg, SEG_W), wv), pev, wv, k_norm[0], B, n_seg, False)
    gate = proj[:, NSA_QW + 6 * NSA_KW:NSA_QW + 6 * NSA_KW + 3 * NSA_HQ]
    gates = gate.reshape(B, L, NSA_HKV, 3 * NSA_G).transpose(0, 2, 1, 3)
    o = nsa_attn(qn, qr, kcmp, vcmp, _cmp_to_slc(n_seg, LANE, n_slc), ks_hm, vs_hm, kw_hm, vw_hm, gates, n_slc)
    n_keep = min(WINDOW, L)
    tok = lambda a: a.reshape(B, L, NSA_HKV, NSA_DH)
    return (mm_res(o.reshape(B * L, NSA_QW), w_out, h),
            (tok(kc), tok(vc), tok(ks), tok(vs), tok(kw)[:, L - n_keep:], tok(vw)[:, L - n_keep:]))


def _nsa_sample(h, proj, B, L, past, page_table, q_norm, k_norm, pe_k, pe_v, w_ck, w_cv, w_out):
    pc_k, pc_v, ps_k, ps_v, buf_k, buf_v = past
    n_pool, page = pc_k.shape[:2]
    n_pages = page_table.shape[1]
    pos0 = n_pages * page
    cos, sin = _rope_tables(pos0, L)
    qn, qr, _, _, _, _, kc, vc, ks, vs, kw, vw = nsa_heads(proj, jnp.tile(cos, (B, 1)), jnp.tile(sin, (B, 1)),
                                                             q_norm, k_norm)
    gate = proj[:, NSA_QW + 6 * NSA_KW:NSA_QW + 6 * NSA_KW + 3 * NSA_HQ]
    assert L <= CMP_STRIDE and page % CMP_STRIDE == 0 and pos0 % SLC_BLOCK == 0
    n_seg = pos0 // CMP_STRIDE
    n_slc = -(-(pos0 + L) // SLC_BLOCK)
    nbl = _round_up(n_slc, LANE)
    wk, pek = _cmp_weights(w_ck, pe_k)
    wv, pev = _cmp_weights(w_cv, pe_v)
    seg_pool = lambda p: p.reshape(n_pool, page // CMP_STRIDE, SEG_W)
    kcmp = cmp_finish(cmp_proj_paged(page_table, seg_pool(pc_k), wk), pek, wk, k_norm[0], B, n_seg, True)
    vcmp = cmp_finish(cmp_proj_paged(page_table, seg_pool(pc_v), wv), pev, wv, k_norm[0], B, n_seg, False)
    rows = lambda a: (a.reshape(NSA_HKV, NSA_G, B, L, NSA_DH).transpose(2, 0, 1, 3, 4)
                      .reshape(B, NSA_HKV, NSA_G * L, NSA_DH))
    flat_pool = lambda p: p.reshape(n_pool, page, NSA_KW)
    new8 = lambda a: jnp.pad(a.reshape(B, L, NSA_KW), ((0, 0), (0, SUBLANE - L), (0, 0)))
    tok = lambda a: a.reshape(B, L, NSA_HKV, NSA_DH)
    wb = buf_k.shape[1]
    gates = gate.reshape(B, L, NSA_HKV, NSA_G, 3).transpose(0, 2, 3, 1, 4).reshape(B, NSA_HKV, NSA_G * L, 3)
    o = sample_attn(page_table, rows(qn), rows(qr), kcmp, vcmp, flat_pool(ps_k), flat_pool(ps_v),
                    new8(ks), new8(vs), buf_k.reshape(B, wb, NSA_KW), buf_v.reshape(B, wb, NSA_KW),
                    new8(kw), new8(vw), gates, _cmp_to_slc(n_seg, nbl, n_slc), L, pos0, n_slc)
    o = o.reshape(B, NSA_HKV, NSA_G, L, NSA_DH).transpose(0, 3, 1, 2, 4).reshape(B * L, NSA_QW)
    win_k = jnp.concatenate([buf_k, tok(kw)], axis=1)[:, L:]
    win_v = jnp.concatenate([buf_v, tok(vw)], axis=1)[:, L:]
    return mm_res(o, w_out, h), (tok(kc), tok(vc), tok(ks), tok(vs), win_k, win_v)


def _pad_cols(w, n):
    return jnp.pad(w, ((0, 0), (0, n - w.shape[1]))).astype(BF16)


def kernel(x_prompt, x_sample, state_delta, state_conv, cache_swa_k, cache_swa_v, cache_cmp_k, cache_cmp_v, cache_slc_k, cache_slc_v, page_table, norm_mix, norm_ffn, dn_w_in, dn_conv_w, dn_a_log, dn_dt_bias, dn_norm, dn_w_out, sg_w_in, sg_ln_g, sg_ln_b, sg_w_spatial, sg_b_spatial, sg_w_out, nsa_w_in, nsa_q_norm, nsa_k_norm, nsa_cmp_pe_k, nsa_cmp_pe_v, nsa_cmp_w_k, nsa_cmp_w_v, nsa_w_out, ffn_w_gate, ffn_w_up, ffn_w_down):
    Bp, Lp, D = x_prompt.shape
    Bs, Ls, _ = x_sample.shape
    depth = norm_mix.shape[0]
    hp = x_prompt.reshape(Bp * Lp, D)
    hs = x_sample.reshape(Bs * Ls, D)
    dn_S_p, dn_S_s, dn_c_p, dn_c_s, sg_v_s = [], [], [], [], []
    c_p = ([], [], [], [], [], [])
    c_s = ([], [], [], [], [], [])
    for i in range(depth):
        kind, j = i % N_MIXERS, i // N_MIXERS
        if kind == 0:
            w_in = _pad_cols(dn_w_in[j], _round_up(dn_w_in.shape[2], LANE))
            w_out = dn_w_out[j].astype(BF16)
            prm = (dn_conv_w[j], dn_a_log[j], dn_dt_bias[j], dn_norm[j], w_out)
            proj_p = rms_matmul(hp, norm_mix[i], w_in)
            proj_s = rms_matmul(hs, norm_mix[i], w_in)
            zero_buf = jnp.zeros((Bp, DN_CONV - 1, DN_CONV_DIM), F32)
            zero_S = jnp.zeros((Bp, DN_HV, DN_DK, DN_DV), F32)
            hp, Sp, cp = _delta_group(hp, proj_p, Bp, Lp, zero_buf, zero_S, *prm)
            hs, Ss, cs = _delta_group(hs, proj_s, Bs, Ls, state_conv[j], state_delta[j], *prm)
            dn_S_p.append(Sp); dn_S_s.append(Ss); dn_c_p.append(cp); dn_c_s.append(cs)
        elif kind == 1:
            W = sg_w_in.shape[2] // 2
            w_in = sg_w_in[j].astype(BF16)
            w_out = sg_w_out[j].astype(BF16)
            prm = (sg_ln_g[j], sg_ln_b[j], sg_w_spatial[j], sg_b_spatial[j])
            y_p = rms_matmul(hp, norm_mix[i], w_in, act="gelu")
            y_s = rms_matmul(hs, norm_mix[i], w_in, act="gelu")
            (a_p,) = sg_mix(y_p, *prm, want_v=False)
            y_s = jnp.pad(y_s.reshape(Bs, Ls, 2 * W), ((0, 0), (0, SG_CHUNK - Ls), (0, 0)))
            a_s, v_s = sg_mix(y_s.reshape(Bs * SG_CHUNK, 2 * W), *prm, want_v=True)
            unpad = lambda a: a.reshape(Bs, SG_CHUNK, W)[:, :Ls]
            hp = mm_res(a_p, w_out, hp)
            hs = mm_res(unpad(a_s).reshape(Bs * Ls, W), w_out, hs)
            sg_v_s.append(unpad(v_s))
        else:
            w_in = _pad_cols(nsa_w_in[j], _round_up(nsa_w_in.shape[2], LANE))
            w_out = nsa_w_out[j].astype(BF16)
            prm = (nsa_q_norm[j], nsa_k_norm[j], nsa_cmp_pe_k[j], nsa_cmp_pe_v[j], nsa_cmp_w_k[j], nsa_cmp_w_v[j], w_out)
            past = (cache_cmp_k[j], cache_cmp_v[j], cache_slc_k[j], cache_slc_v[j], cache_swa_k[j], cache_swa_v[j])
            proj_p = rms_matmul(hp, norm_mix[i], w_in)
            proj_s = rms_matmul(hs, norm_mix[i], w_in)
            hp, new_p = _nsa_prompt(hp, proj_p, Bp, Lp, *prm)
            hs, new_s = _nsa_sample(hs, proj_s, Bs, Ls, past, page_table, *prm)
            for lst, a in zip(c_p, new_p):
                lst.append(a)
            for lst, a in zip(c_s, new_s):
                lst.append(a)
        wg, wu, wd = ffn_w_gate[i].astype(BF16), ffn_w_up[i].astype(BF16), ffn_w_down[i].astype(BF16)
        hp = ffn(hp, norm_ffn[i], wg, wu, wd)
        hs = ffn(hs, norm_ffn[i], wg, wu, wd)
    return (hp.reshape(Bp, Lp, D), hs.reshape(Bs, Ls, D),
            jnp.stack(dn_S_p), jnp.stack(dn_S_s), jnp.stack(dn_c_p), jnp.stack(dn_c_s),
            jnp.stack(sg_v_s),
            jnp.stack(c_p[4]), jnp.stack(c_p[5]), jnp.stack(c_s[4]), jnp.stack(c_s[5]),
            jnp.stack(c_p[0]), jnp.stack(c_p[1]), jnp.stack(c_p[2]), jnp.stack(c_p[3]),
            jnp.stack(c_s[0]), jnp.stack(c_s[1]), jnp.stack(c_s[2]), jnp.stack(c_s[3]))
```

```python
import functools
import math

import numpy as np
import jax
import jax.numpy as jnp
from jax import lax
from jax.experimental import pallas as pl
from jax.experimental.pallas import tpu as pltpu

F32 = jnp.float32
BF16 = jnp.bfloat16
HIGHEST = lax.Precision.HIGHEST

N_MIXERS = 3
NORM_EPS = 1e-6
LN_EPS = 1e-5
L2_EPS = 1e-6
DN_DK = 128
DN_DV = 128
DN_HK = 8
DN_HV = 16
DN_CONV = 4
DN_CHUNK = 64
INV_BLOCK = 16
DN_QK = DN_HK * DN_DK
DN_VW = DN_HV * DN_DV
DN_CONV_DIM = 2 * DN_QK + DN_VW
DN_HEADS_ALL = DN_CONV_DIM // DN_DK
SG_CHUNK = 128
SG_GROUPS = 8
NSA_DH = 64
NSA_HQ = 16
NSA_HKV = 4
NSA_G = NSA_HQ // NSA_HKV
CMP_STRIDE = 16
CMP_BLOCK = 32
SLC_BLOCK = 64
N_SEL = 16
WINDOW = 512
Q_BLOCK = 128
FORCE_BONUS = 1e4
ROPE_THETA = 10000.0
NSA_QW = NSA_HQ * NSA_DH
NSA_KW = NSA_HKV * NSA_DH
SEG_W = CMP_STRIDE * NSA_KW
NEG = -1e30
QK_SCALE_LOG2 = NSA_DH ** -0.5 * math.log2(math.e)

LANE = 128
SUBLANE = 8
VMEM_LIMIT = 56 * 1024 * 1024


def _cparams(sem):
    return pltpu.CompilerParams(dimension_semantics=sem, vmem_limit_bytes=VMEM_LIMIT)


def _round_up(n, m):
    return -(-n // m) * m


def _pick(n, cands):
    for c in cands:
        if n % c == 0:
            return c
    raise ValueError(f"no tile for {n} in {cands}")


def _dot(a, b):
    return jnp.dot(a, b, preferred_element_type=F32)


def _dot_nt(a, b, precision=None):
    return lax.dot_general(a, b, (((1,), (1,)), ((), ())), precision=precision, preferred_element_type=F32)


def _dotf(a, b):
    return jnp.dot(a, b, precision=HIGHEST, preferred_element_type=F32)


def _split_bf16(a):
    hi = a.astype(BF16)
    return hi, (a - hi.astype(F32)).astype(BF16)


def _dot3_nt(a, b):
    ah, al = _split_bf16(a)
    bh, bl = _split_bf16(b)
    return _dot_nt(ah, bh) + (_dot_nt(ah, bl) + _dot_nt(al, bh))


def _rms(x, g, eps=NORM_EPS):
    return x * lax.rsqrt(jnp.mean(x * x, -1, keepdims=True) + eps) * g


def _resident(shape):
    nd = len(shape)
    return pl.BlockSpec(shape, lambda *_: (0,) * nd, pipeline_mode=pl.Buffered(1))


def _rms_matmul_body(x_ref, g_ref, w_ref, o_ref, *, cn, act):
    xb = _rms(x_ref[...], g_ref[...]).astype(BF16)
    for c0 in range(0, o_ref.shape[-1], cn):
        y = _dot(xb, w_ref[:, c0:c0 + cn])
        if act == "gelu":
            y = jax.nn.gelu(y)
        o_ref[:, c0:c0 + cn] = y


def rms_matmul(x, g, w, *, act=None):
    T, D = x.shape
    N = w.shape[1]
    tm = _pick(T, (256, 128, 64, 32, 16, 8))
    k = 1
    while not (N % k == 0 and (N // k) % LANE == 0 and N // k <= 1024):
        k += 1
    return pl.pallas_call(
        functools.partial(_rms_matmul_body, cn=N // k, act=act),
        out_shape=jax.ShapeDtypeStruct((T, N), F32),
        grid=(T // tm,),
        in_specs=[pl.BlockSpec((tm, D), lambda i: (i, 0)), _resident((1, D)), _resident((D, N))],
        out_specs=pl.BlockSpec((tm, N), lambda i: (i, 0)),
        compiler_params=_cparams(("parallel",)),
        name="rms_matmul",
    )(x, g.reshape(1, D), w)


def _ffn_body(x_ref, g_ref, wg_ref, wu_ref, wd_ref, o_ref, *, hc):
    x = x_ref[...]
    xb = _rms(x, g_ref[...]).astype(BF16)
    acc = x
    for c0 in range(0, wg_ref.shape[1], hc):
        gate = _dot(xb, wg_ref[:, c0:c0 + hc])
        up = _dot(xb, wu_ref[:, c0:c0 + hc])
        a = (jax.nn.silu(gate) * up).astype(BF16)
        acc = acc + _dot(a, wd_ref[c0:c0 + hc, :])
    o_ref[...] = acc


def ffn(x, g, wg, wu, wd):
    T, D = x.shape
    H = wg.shape[1]
    tm = _pick(T, (512, 256, 128, 64, 32, 16, 8))
    hc = H // 2 if (H // 2) % LANE == 0 else H
    return pl.pallas_call(
        functools.partial(_ffn_body, hc=hc),
        out_shape=jax.ShapeDtypeStruct((T, D), F32),
        grid=(T // tm,),
        in_specs=[pl.BlockSpec((tm, D), lambda i: (i, 0)), _resident((1, D)),
                  _resident((D, H)), _resident((D, H)), _resident((H, D))],
        out_specs=pl.BlockSpec((tm, D), lambda i: (i, 0)),
        compiler_params=_cparams(("parallel",)),
        name="ffn",
    )(x, g.reshape(1, D), wg, wu, wd)


def _mm_res_body(a_ref, w_ref, r_ref, o_ref):
    o_ref[...] = r_ref[...] + _dot(a_ref[...].astype(BF16), w_ref[...])


def mm_res(a, w, res):
    T, K = a.shape
    D = w.shape[1]
    tm = _pick(T, (512, 256, 128, 64, 32, 16, 8))
    return pl.pallas_call(
        _mm_res_body,
        out_shape=jax.ShapeDtypeStruct((T, D), F32),
        grid=(T // tm,),
        in_specs=[pl.BlockSpec((tm, K), lambda i: (i, 0)), _resident((K, D)),
                  pl.BlockSpec((tm, D), lambda i: (i, 0))],
        out_specs=pl.BlockSpec((tm, D), lambda i: (i, 0)),
        compiler_params=_cparams(("parallel",)),
        name="mm_res",
    )(a, w, res)


def _dn_gates_body(p_ref, alog_ref, dtb_ref, beta_ref, g_ref):
    p = p_ref[...]
    beta_ref[...] = jax.nn.sigmoid(p[:, :DN_HV])
    g_ref[...] = -jnp.exp(alog_ref[...]) * jax.nn.softplus(p[:, DN_HV:2 * DN_HV] + dtb_ref[...])


def dn_gates(proj, a_log, dt_bias):
    T = proj.shape[0]
    tm = _pick(T, (512, 256, 128, 64, 32, 16, 8))
    col_blk = (DN_CONV_DIM + DN_VW) // LANE
    return pl.pallas_call(
        _dn_gates_body,
        out_shape=(jax.ShapeDtypeStruct((T, DN_HV), F32),) * 2,
        grid=(T // tm,),
        in_specs=[pl.BlockSpec((tm, LANE), lambda i: (i, col_blk)), _resident((1, DN_HV)), _resident((1, DN_HV))],
        out_specs=(pl.BlockSpec((tm, DN_HV), lambda i: (i, 0)),) * 2,
        compiler_params=_cparams(("parallel",)),
        name="dn_gates",
    )(proj, a_log.reshape(1, DN_HV), dt_bias.reshape(1, DN_HV))


def _dn_conv_body(x_ref, prev_ref, first_ref, w_ref, o_ref, *, nt):
    seq_start = lax.rem(pl.program_id(0), nt) == 0
    for h in range(DN_HEADS_ALL):
        sl = slice(h * DN_DK, (h + 1) * DN_DK)
        x = x_ref[:, sl]
        xc = jnp.concatenate([jnp.where(seq_start, first_ref[0, :, sl], prev_ref[:, sl]), x], axis=0)
        w = w_ref[:, sl]
        acc = x * w[DN_CONV - 1:DN_CONV]
        for k in range(1, DN_CONV):
            shifted = pltpu.roll(xc, k, 0)[SUBLANE:]
            acc = acc + shifted * w[DN_CONV - 1 - k:DN_CONV - k]
        c = jax.nn.silu(acc)
        if h < 2 * DN_HK:
            c = c * lax.rsqrt(jnp.sum(c * c, -1, keepdims=True) + L2_EPS)
        if h < DN_HK:
            c = c * DN_DK ** -0.5
        o_ref[h] = c


def dn_conv(x, first, conv_w, rows, tt, nt):
    per8 = tt // SUBLANE
    return pl.pallas_call(
        functools.partial(_dn_conv_body, nt=nt),
        out_shape=jax.ShapeDtypeStruct((DN_HEADS_ALL, rows, DN_DK), F32),
        grid=(rows // tt,),
        in_specs=[pl.BlockSpec((tt, DN_CONV_DIM), lambda i: (i, 0)),
                  pl.BlockSpec((SUBLANE, DN_CONV_DIM), lambda i: (jnp.maximum(i * per8 - 1, 0), 0)),
                  pl.BlockSpec((1, SUBLANE, DN_CONV_DIM), lambda i: (i // nt, 0, 0)),
                  _resident((DN_CONV, DN_CONV_DIM))],
        out_specs=pl.BlockSpec((DN_HEADS_ALL, tt, DN_DK), lambda i: (0, i, 0)),
        compiler_params=_cparams(("parallel",)),
        name="dn_conv",
    )(x, x, first, conv_w)


def _bdot(a, b):
    return lax.dot_general(a, b, (((2,), (1,)), ((0,), (0,))), preferred_element_type=F32)


def _bdot_nt(a, b):
    return lax.dot_general(a, b, (((2,), (2,)), ((0,), (0,))), preferred_element_type=F32)


def _bdot3(a, b):
    ah, al = _split_bf16(a)
    bh, bl = _split_bf16(b)
    return _bdot(ah, bh) + (_bdot(ah, bl) + _bdot(al, bh))


def _delta_body(q_ref, k_ref, v_ref, g_ref, b_ref, s0_ref, o_ref, sout_ref, S_ref, *, hb):
    c = pl.program_id(1)
    C = DN_CHUNK

    @pl.when(c == 0)
    def _():
        S_ref[...] = s0_ref[0]

    ri = lax.broadcasted_iota(jnp.int32, (C, C), 0)
    ci = lax.broadcasted_iota(jnp.int32, (C, C), 1)
    tri = ri >= ci
    stri = ri > ci
    eye = jnp.where(ri == ci, 1.0, 0.0).astype(F32)
    bi, bj = ri // INV_BLOCK, ci // INV_BLOCK
    rep = DN_HV // DN_HK
    nq = hb // rep
    q = q_ref[...][:, None]
    k = k_ref[...][:, None]
    kb16 = k_ref[...].astype(BF16)
    kk = _bdot_nt(kb16, kb16)[:, None]
    qk0 = _bdot_nt(q_ref[...].astype(BF16), kb16)[:, None]
    grow = g_ref[...].reshape(nq, rep, 1, C)
    brow = b_ref[...].reshape(nq, rep, 1, C)
    gcol = jnp.sum(eye * grow, -1, keepdims=True)
    bcol = jnp.sum(eye * brow, -1, keepdims=True)
    gc_col = jnp.sum(jnp.where(tri, grow, 0.0), -1, keepdims=True)
    gc_row = jnp.sum(jnp.where(ri <= ci, gcol, 0.0), -2, keepdims=True)
    decay = jnp.exp(jnp.where(tri, gc_col - gc_row, NEG))
    a_mat = (jnp.where(stri, kk * decay, 0.0) * bcol).reshape(hb, C, C)
    eg = jnp.exp(gc_col)
    v = v_ref[...].reshape(nq, rep, C, DN_DV)
    rhs = jnp.concatenate([v * bcol, k * (bcol * eg)], axis=-1).reshape(hb, C, 2 * DN_DV)
    p = -jnp.where(bi == bj, a_mat, 0.0)
    t = eye + p
    for _ in range(int(math.log2(INV_BLOCK)) - 1):
        p = _bdot3(p, p)
        t = t + _bdot3(t, p)
    width = INV_BLOCK
    while width < C:
        e = jnp.where((bi // (2 * width // INV_BLOCK) == bj // (2 * width // INV_BLOCK))
                      & (bi // (width // INV_BLOCK) != bj // (width // INV_BLOCK)), a_mat, 0.0)
        t = t - _bdot3(_bdot3(t, e), t)
        width *= 2
    sol = _bdot3(t, rhs)
    g_last = gc_col[:, :, C - 1:C]
    u = sol[:, :, :DN_DV]
    wq = jnp.concatenate([sol[:, :, DN_DV:], (q * eg).reshape(hb, C, DN_DK)], axis=1).astype(BF16)
    qk = jnp.where(tri, qk0 * decay, 0.0).reshape(hb, C, C).astype(BF16)
    kdec_t = jnp.swapaxes((k * jnp.exp(g_last - gc_col)).reshape(hb, C, DN_DK), 1, 2).astype(BF16)
    g_keep = jnp.exp(g_last).reshape(hb, 1, 1)

    S = S_ref[...]
    ws = _bdot(wq, S.astype(BF16))
    vb = (u - ws[:, :C]).astype(BF16)
    o_ref[...] = ws[:, C:] + _bdot(qk, vb)
    S_ref[...] = S * g_keep + _bdot(kdec_t, vb)

    @pl.when(c == pl.num_programs(1) - 1)
    def _():
        sout_ref[0] = S_ref[...]


def delta_rule(qkv_hm, g_rows, b_rows, s0, B, n):
    hb = DN_HV
    C = DN_CHUNK
    state_blk = pl.BlockSpec((1, hb, DN_DK, DN_DV), lambda b, c: (b, 0, 0, 0))
    gate_blk = pl.BlockSpec((hb, 1, 1, C), lambda b, c: (0, b * n + c, 0, 0))
    return pl.pallas_call(
        functools.partial(_delta_body, hb=hb),
        out_shape=(jax.ShapeDtypeStruct((DN_HV, B * n * C, DN_DV), F32),
                   jax.ShapeDtypeStruct((B, DN_HV, DN_DK, DN_DV), F32)),
        grid=(B, n),
        in_specs=[
            pl.BlockSpec((DN_HK, C, DN_DK), lambda b, c: (0, b * n + c, 0)),
            pl.BlockSpec((DN_HK, C, DN_DK), lambda b, c: (1, b * n + c, 0)),
            pl.BlockSpec((DN_HV, C, DN_DV), lambda b, c: (1, b * n + c, 0)),
            gate_blk, gate_blk, state_blk,
        ],
        out_specs=(pl.BlockSpec((hb, C, DN_DV), lambda b, c: (0, b * n + c, 0)), state_blk),
        scratch_shapes=[pltpu.VMEM((hb, DN_DK, DN_DV), F32)],
        compiler_params=_cparams(("parallel", "arbitrary")),
        name="delta_rule",
    )(qkv_hm, qkv_hm, qkv_hm, g_rows, b_rows, s0)


def _dn_out_body(o_ref, z_ref, g_ref, w_ref, r_ref, out_ref):
    parts = []
    for h in range(DN_HV):
        z = z_ref[:, h * DN_DV:(h + 1) * DN_DV]
        parts.append((_rms(o_ref[h], g_ref[...]) * jax.nn.silu(z)).astype(BF16))
    out_ref[...] = r_ref[...] + _dot(jnp.concatenate(parts, axis=-1), w_ref[...])


def dn_out(o_hm, proj, norm_g, w_out, res):
    T, D = res.shape
    tm = _pick(T, (256, 128, 64, 32, 16, 8))
    return pl.pallas_call(
        _dn_out_body,
        out_shape=jax.ShapeDtypeStruct((T, D), F32),
        grid=(T // tm,),
        in_specs=[pl.BlockSpec((DN_HV, tm, DN_DV), lambda i: (0, i, 0)),
                  pl.BlockSpec((tm, DN_VW), lambda i: (i, DN_CONV_DIM // DN_VW)),
                  _resident((1, DN_DV)), _resident((DN_VW, D)),
                  pl.BlockSpec((tm, D), lambda i: (i, 0))],
        out_specs=pl.BlockSpec((tm, D), lambda i: (i, 0)),
        compiler_params=_cparams(("parallel",)),
        name="dn_out",
    )(o_hm, proj, norm_g.reshape(1, DN_DV), w_out, res)


def _delta_group(h, proj, B, L, conv_buf, s0, conv_w, a_log, dt_bias, norm_g, w_out):
    C = DN_CHUNK
    Lp = _round_up(L, C)
    n = Lp // C
    tail = proj.reshape(B, L, proj.shape[1])[:, max(L - (DN_CONV - 1), 0):, :DN_CONV_DIM]
    new_buf = jnp.concatenate([conv_buf, tail], axis=1)[:, -(DN_CONV - 1):]
    tt = _pick(Lp, (256, 128, 64))
    nt = Lp // tt
    if Lp == L:
        x = proj
    else:
        qkv = proj[:, :DN_CONV_DIM].reshape(B, L, DN_CONV_DIM)
        x = jnp.pad(qkv, ((0, 0), (0, Lp - L), (0, 0))).reshape(B * Lp, DN_CONV_DIM)
    first = jnp.pad(conv_buf, ((0, 0), (SUBLANE - (DN_CONV - 1), 0), (0, 0)))
    qkv_hm = dn_conv(x, first, conv_w, B * Lp, tt, nt)
    beta, g = dn_gates(proj, a_log, dt_bias)

    def rows(a):
        a = jnp.pad(a.reshape(B, L, DN_HV), ((0, 0), (0, Lp - L), (0, 0)))
        return a.reshape(B, n, C, DN_HV).transpose(3, 0, 1, 2).reshape(DN_HV, B * n, 1, C)

    o_hm, S = delta_rule(qkv_hm, rows(g), rows(beta), s0, B, n)
    if Lp != L:
        o_hm = o_hm.reshape(DN_HV, B, Lp, DN_DV)[:, :, :L].reshape(DN_HV, B * L, DN_DV)
    return dn_out(o_hm, proj, norm_g, w_out, h), S, new_buf


def _sg_body(y_ref, lng_ref, lnb_ref, wsp_ref, bsp_ref, a_ref, *v_out, width):
    u = y_ref[:, :width]
    v = y_ref[:, width:]
    xc = v - jnp.mean(v, -1, keepdims=True)
    vn = xc * lax.rsqrt(jnp.mean(xc * xc, -1, keepdims=True) + LN_EPS) * lng_ref[...] + lnb_ref[...]
    if v_out:
        v_out[0][...] = vn
    n = wsp_ref.shape[1]
    gw = width // SG_GROUPS
    tril = lax.broadcasted_iota(jnp.int32, (n, n), 0) >= lax.broadcasted_iota(jnp.int32, (n, n), 1)
    for g in range(SG_GROUPS):
        sl = slice(g * gw, (g + 1) * gw)
        wt = jnp.where(tril, wsp_ref[g], 0.0).astype(BF16)
        mixed = _dot(wt, vn[:, sl].astype(BF16)) + bsp_ref[:, g:g + 1]
        a_ref[:, sl] = u[:, sl] * mixed


def sg_mix(y, ln_g, ln_b, w_sp, b_sp, want_v):
    T = y.shape[0]
    W = y.shape[1] // 2
    n = SG_CHUNK
    out_shape = [jax.ShapeDtypeStruct((T, W), F32)]
    out_specs = [pl.BlockSpec((n, W), lambda i: (i, 0))]
    if want_v:
        out_shape.append(jax.ShapeDtypeStruct((T, W), F32))
        out_specs.append(pl.BlockSpec((n, W), lambda i: (i, 0)))
    return pl.pallas_call(
        functools.partial(_sg_body, width=W),
        out_shape=tuple(out_shape),
        grid=(T // n,),
        in_specs=[pl.BlockSpec((n, 2 * W), lambda i: (i, 0)), _resident((1, W)), _resident((1, W)),
                  _resident((SG_GROUPS, n, n)), _resident((n, SG_GROUPS))],
        out_specs=tuple(out_specs),
        compiler_params=_cparams(("parallel",)),
        name="sg_mix",
    )(y, ln_g.reshape(1, W), ln_b.reshape(1, W), w_sp, b_sp.T)


def _nsa_heads_body(p_ref, cos_ref, sin_ref, qg_ref, kg_ref, qn_ref, qr_ref, ksh_ref, vsh_ref, kwh_ref, vwh_ref,
                    kc_ref, vc_ref, kst_ref, vst_ref, kwt_ref, vwt_ref):
    cos, sin = cos_ref[...], sin_ref[...]
    half = NSA_DH // 2

    def head(col0, h):
        return p_ref[:, col0 + h * NSA_DH:col0 + (h + 1) * NSA_DH]

    def rope(xn):
        return xn * cos + jnp.concatenate([xn[:, half:], xn[:, :half]], axis=-1) * sin

    for h in range(NSA_HQ):
        xn = _rms(head(0, h), qg_ref[...])
        qn_ref[h] = xn
        qr_ref[h] = rope(xn)
    col = NSA_QW
    kc_ref[...] = p_ref[:, col:col + NSA_KW]
    vc_ref[...] = p_ref[:, col + NSA_KW:col + 2 * NSA_KW]
    col += 2 * NSA_KW
    for kind, (kh_ref, kt_ref, vh_ref, vt_ref) in enumerate(((ksh_ref, kst_ref, vsh_ref, vst_ref),
                                                             (kwh_ref, kwt_ref, vwh_ref, vwt_ref))):
        parts = []
        for h in range(NSA_HKV):
            r = rope(_rms(head(col, h), kg_ref[kind + 1:kind + 2]))
            kh_ref[h] = r.astype(BF16)
            parts.append(r)
            vh_ref[h] = head(col + NSA_KW, h).astype(BF16)
        kt_ref[...] = jnp.concatenate(parts, axis=-1)
        vt_ref[...] = p_ref[:, col + NSA_KW:col + 2 * NSA_KW]
        col += 2 * NSA_KW


def nsa_heads(proj, cos, sin, q_norm, k_norm):
    T, N = proj.shape
    Tc = cos.shape[0]
    tl = _pick(Tc, (256, 128, 64, 32, 16, 8))
    n_tab = Tc // tl
    tab = pl.BlockSpec((tl, NSA_DH), lambda i: (i % n_tab, 0))
    hm = lambda H: pl.BlockSpec((H, tl, NSA_DH), lambda i: (0, i, 0))
    tok = pl.BlockSpec((tl, NSA_KW), lambda i: (i, 0))
    return pl.pallas_call(
        _nsa_heads_body,
        out_shape=((jax.ShapeDtypeStruct((NSA_HQ, T, NSA_DH), F32),) * 2
                   + (jax.ShapeDtypeStruct((NSA_HKV, T, NSA_DH), BF16),) * 4
                   + (jax.ShapeDtypeStruct((T, NSA_KW), F32),) * 6),
        grid=(T // tl,),
        in_specs=[pl.BlockSpec((tl, N), lambda i: (i, 0)), tab, tab, _resident((1, NSA_DH)), _resident((3, NSA_DH))],
        out_specs=(hm(NSA_HQ),) * 2 + (hm(NSA_HKV),) * 4 + (tok,) * 6,
        compiler_params=_cparams(("parallel",)),
        name="nsa_heads",
    )(proj, cos, sin, q_norm.reshape(1, NSA_DH), k_norm)


def _cmp_proj_body(*refs):
    x_refs, w_ref, o_ref = refs[:-2], refs[-2], refs[-1]
    n = o_ref.shape[0]

    def tap(l):
        x = jnp.concatenate([r[pl.ds(l, n, stride=CMP_STRIDE), :] for r in x_refs], axis=-1)
        return _dot(x.astype(BF16), w_ref[l])

    acc = tap(0)
    for l in range(1, CMP_STRIDE):
        acc = acc + tap(l)
    o_ref[...] = acc


def cmp_proj(x, wtaps):
    rows = x.shape[0]
    segs = rows // CMP_STRIDE
    tm = _pick(segs, (256, 128, 64, 32, 16, 8))
    return pl.pallas_call(
        _cmp_proj_body,
        out_shape=jax.ShapeDtypeStruct((segs, 2 * NSA_KW), F32),
        grid=(segs // tm,),
        in_specs=[pl.BlockSpec((tm * CMP_STRIDE, LANE), lambda i, c=c: (i, c)) for c in range(NSA_KW // LANE)]
                 + [_resident((CMP_STRIDE, NSA_KW, 2 * NSA_KW))],
        out_specs=pl.BlockSpec((tm, 2 * NSA_KW), lambda i: (i, 0)),
        compiler_params=_cparams(("parallel",)),
        name="cmp_proj",
    )(*([x] * (NSA_KW // LANE)), wtaps)


def _page_copies(pool_ref, buf_ref, sem_ref, page, slot, p):
    if buf_ref.ndim == pool_ref.ndim + 1:
        return [pltpu.make_async_copy(pool_ref.at[page], buf_ref.at[slot, p], sem_ref.at[slot])]
    return [pltpu.make_async_copy(pool_ref.at[page, :, pl.ds(t * LANE, LANE)], buf_ref.at[t, slot, p], sem_ref.at[slot])
            for t in range(buf_ref.shape[0])]


def _fetch_pages(pt_ref, pool_ref, buf_ref, sem_ref, b, chunk, slot, pch):
    for p in range(pch):
        for cp in _page_copies(pool_ref, buf_ref, sem_ref, pt_ref[b, chunk * pch + p], slot, p):
            cp.start()


def _wait_pages(pool_ref, buf_ref, sem_ref, slot, pch):
    for p in range(pch):
        for cp in _page_copies(pool_ref, buf_ref, sem_ref, 0, slot, p):
            cp.wait()


def _cmp_proj_paged_body(pt_ref, pool_ref, w_ref, o_ref, buf_ref, sem_ref, *, pch):
    b, ch = pl.program_id(0), pl.program_id(1)
    nb, nch = pl.num_programs(0), pl.num_programs(1)
    step = b * nch + ch
    slot = lax.rem(step, 2)

    @pl.when(step == 0)
    def _():
        _fetch_pages(pt_ref, pool_ref, buf_ref, sem_ref, b, ch, slot, pch)

    @pl.when(step + 1 < nb * nch)
    def _():
        nxt = step + 1
        _fetch_pages(pt_ref, pool_ref, buf_ref, sem_ref, nxt // nch, lax.rem(nxt, nch), 1 - slot, pch)

    _wait_pages(pool_ref, buf_ref, sem_ref, slot, pch)
    tiles, _, _, page, _ = buf_ref.shape
    segs = page // CMP_STRIDE

    def tap(l):
        x = jnp.concatenate([buf_ref[t, slot, :, pl.ds(l, segs, stride=CMP_STRIDE), :] for t in range(tiles)],
                            axis=-1)
        return _dot(x.reshape(pch * segs, NSA_KW).astype(BF16), w_ref[l])

    acc = tap(0)
    for l in range(1, CMP_STRIDE):
        acc = acc + tap(l)
    o_ref[...] = acc


def cmp_proj_paged(page_table, pool, wtaps):
    B, n_pages = page_table.shape
    page = pool.shape[1]
    segs = page // CMP_STRIDE
    pch = _pick(n_pages, (32, 16, 8, 4, 2, 1))
    nch = n_pages // pch
    return pl.pallas_call(
        functools.partial(_cmp_proj_paged_body, pch=pch),
        out_shape=jax.ShapeDtypeStruct((B * n_pages * segs, 2 * NSA_KW), F32),
        grid_spec=pltpu.PrefetchScalarGridSpec(
            num_scalar_prefetch=1,
            grid=(B, nch),
            in_specs=[pl.BlockSpec(memory_space=pl.ANY),
                      pl.BlockSpec((CMP_STRIDE, NSA_KW, 2 * NSA_KW), lambda b, c, pt: (0, 0, 0))],
            out_specs=pl.BlockSpec((pch * segs, 2 * NSA_KW), lambda b, c, pt: (b * nch + c, 0)),
            scratch_shapes=[pltpu.VMEM((NSA_KW // LANE, 2, pch, page, LANE), F32), pltpu.SemaphoreType.DMA((2,))],
        ),
        compiler_params=_cparams(("arbitrary", "arbitrary")),
        name="cmp_proj_paged",
    )(page_table, pool, wtaps)


def _cmp_finish_body(a_ref, pe_ref, w_ref, g_ref, o_ref, *, do_norm):
    a = a_ref[0]
    n = a.shape[0]
    pe = _dot(pe_ref[0].astype(BF16), w_ref[0])
    for l in range(1, CMP_STRIDE):
        pe = pe + _dot(pe_ref[l].astype(BF16), w_ref[l])
    pe_term = pe[0:1, :NSA_KW] + pe[1:2, NSA_KW:]
    blk = a[:, :NSA_KW] + pltpu.roll(a[:, NSA_KW:], n - 1, 0) + pe_term
    for h in range(NSA_HKV):
        bh = blk[:, h * NSA_DH:(h + 1) * NSA_DH]
        if do_norm:
            bh = _rms(bh, g_ref[...])
        o_ref[0, h] = bh


def cmp_finish(a, pe_rows, wtaps, g, B, n, do_norm):
    return pl.pallas_call(
        functools.partial(_cmp_finish_body, do_norm=do_norm),
        out_shape=jax.ShapeDtypeStruct((B, NSA_HKV, n, NSA_DH), F32),
        grid=(B,),
        in_specs=[pl.BlockSpec((1, n, 2 * NSA_KW), lambda b: (b, 0, 0)), _resident((CMP_STRIDE, SUBLANE, NSA_KW)),
                  _resident((CMP_STRIDE, NSA_KW, 2 * NSA_KW)), _resident((1, NSA_DH))],
        out_specs=pl.BlockSpec((1, NSA_HKV, n, NSA_DH), lambda b: (b, 0, 0, 0)),
        compiler_params=_cparams(("parallel",)),
        name="cmp_finish",
    )(a.reshape(B, n, 2 * NSA_KW), pe_rows, wtaps, g.reshape(1, NSA_DH))


def _masked_softmax(s, mask):
    sm = jnp.where(mask, s, NEG)
    m = jnp.max(sm, -1, keepdims=True)
    p = jnp.where(mask, jnp.exp(s - m), 0.0)
    d = jnp.sum(p, -1, keepdims=True)
    return p / jnp.where(d > 0, d, 1.0)


def _topk_mask(score, k, sel):
    nl = score.shape[-1]
    lane = lax.broadcasted_iota(jnp.int32, score.shape, 1).astype(F32)
    work = score
    for _ in range(k):
        m = jnp.max(work, -1, keepdims=True)
        idx = jnp.min(jnp.where(work == m, lane, float(nl)), -1, keepdims=True)
        hit = lane == idx
        sel = jnp.where(hit, 1.0, sel)
        work = jnp.where(hit, -jnp.inf, work)
    return sel


def _select_blocks(imp, qpos, n_lanes_valid):
    blk = lax.broadcasted_iota(jnp.int32, imp.shape, 1)
    cur = qpos // SLC_BLOCK
    eligible = (blk * SLC_BLOCK <= qpos) & (blk < n_lanes_valid)
    forced = (blk == 0) | (blk == cur) | (blk == cur - 1)
    assert NSA_G < FORCE_BONUS
    sel = jnp.where(eligible & forced, 1.0, 0.0)
    return _topk_mask(jnp.where(eligible & ~forced, imp, -jnp.inf), N_SEL - 3, sel)


def _flash_tile(q3, k, v, mask, carry):
    return _flash_tile_bias(q3, k, v, jnp.where(mask, 0.0, NEG), carry)


def _flash_tile_bias(q3, k, v, bias, carry):
    m, l, acc = carry
    G, R, dh = q3.shape
    tk = k.shape[0]
    s = _dot_nt(q3.reshape(G * R, dh), k).reshape(G, R, tk) + bias[None]
    m_new = jnp.maximum(m, jnp.max(s, -1, keepdims=True))
    alpha = jnp.exp2(m - m_new)
    p = jnp.exp2(s - m_new)
    l = alpha * l + jnp.sum(p, -1, keepdims=True)
    pv = _dot(p.reshape(G * R, tk).astype(BF16), v).reshape(G, R, dh)
    return m_new, l, alpha * acc + pv


def _flash_tile_t(q3, k_t, v_t, bias, carry):
    m, l, acc = carry
    G, R, dh = q3.shape
    tk = k_t.shape[1]
    s = _dot(q3.reshape(G * R, dh), k_t).reshape(G, R, tk) + bias[None]
    m_new = jnp.maximum(m, jnp.max(s, -1, keepdims=True))
    alpha = jnp.exp2(m - m_new)
    p = jnp.exp2(s - m_new)
    l = alpha * l + jnp.sum(p, -1, keepdims=True)
    pv = _dot_nt(p.reshape(G * R, tk).astype(BF16), v_t).reshape(G, R, dh)
    return m_new, l, alpha * acc + pv


def _flash_init(G, R, dh):
    return (jnp.full((G, R, 1), NEG, F32), jnp.zeros((G, R, 1), F32), jnp.zeros((G, R, dh), F32))


def _flash_done(carry):
    m, l, acc = carry
    return jnp.where(m > 0.5 * NEG, acc / l, 0.0)


def _cmp_branch(qn, kc, vc, overlap, qpos, n_slc):
    G, qb, dh = qn.shape
    ncp = kc.shape[0]
    s = _dot3_nt(qn.reshape(G * qb, dh) * NSA_DH ** -0.5, kc).reshape(G, qb, ncp)
    kend = lax.broadcasted_iota(jnp.int32, (1, ncp), 1) * CMP_STRIDE + (CMP_BLOCK - 1)
    p = _masked_softmax(s, (kend <= qpos)[None])
    o_cmp = _dot(p.reshape(G * qb, ncp).astype(BF16), vc.astype(BF16)).reshape(G, qb, dh)
    psum = p[0]
    for g in range(1, G):
        psum = psum + p[g]
    ps_hi, ps_lo = _split_bf16(psum)
    sel = _select_blocks(_dot(ps_hi, overlap) + _dot(ps_lo, overlap), qpos, n_slc)
    return o_cmp, jnp.where(sel > 0.5, 0.0, NEG)


def _nsa_attn_body(qn_ref, qnn_ref, q_ref, kc_ref, vc_ref, m_ref, ks_ref, vs_ref, kw_ref, vw_ref, gt_ref, o_ref,
                   ocmp_s, selb_s, *, tk, n_slc):
    i = pl.program_id(2)
    qb = q_ref.shape[1]
    nb = m_ref.shape[1]
    row = lax.broadcasted_iota(jnp.int32, (qb, 1), 0)
    qpos = i * qb + row
    kc, vc = kc_ref[0, 0], vc_ref[0, 0]
    overlap = m_ref[...].astype(BF16)

    @pl.when(i == 0)
    def _():
        ocmp_s[...], selb_s[...] = _cmp_branch(qn_ref[...], kc, vc, overlap, row, n_slc)

    o_cmp, sel_bias = ocmp_s[...], selb_s[...]
    ocmp_next, selb_next = _cmp_branch(qnn_ref[...], kc, vc, overlap, qpos + qb, n_slc)

    q3 = (q_ref[...] * QK_SCALE_LOG2).astype(BF16)
    slab = WINDOW + qb
    start = pl.multiple_of(jnp.maximum(i * qb - WINDOW, 0), qb)
    d = qpos - (start + lax.broadcasted_iota(jnp.int32, (1, slab), 1))
    o_swa = _flash_done(_flash_tile(q3, kw_ref[0, pl.ds(start, slab), :], vw_ref[0, pl.ds(start, slab), :],
                                    (d >= 0) & (d <= WINDOW), _flash_init(NSA_G, qb, NSA_DH)))

    bpt = tk // SLC_BLOCK
    expand = jnp.where(lax.broadcasted_iota(jnp.int32, (nb, tk), 0)
                       == lax.broadcasted_iota(jnp.int32, (nb, tk), 1) // SLC_BLOCK, 1.0, 0.0).astype(BF16)

    def key_bias(j):
        rolled = pltpu.roll(sel_bias, lax.rem(nb - j * bpt, nb), 1)
        return _dot(rolled.astype(BF16), expand)

    def slc_step(j, carry):
        start = pl.multiple_of(j * tk, tk)
        return _flash_tile_bias(q3, ks_ref[0, pl.ds(start, tk), :], vs_ref[0, pl.ds(start, tk), :],
                                key_bias(j), carry)

    assert tk % qb == 0
    n_full = (i * qb) // tk
    start = pl.multiple_of(n_full * tk, tk)
    kpos = n_full * tk + lax.broadcasted_iota(jnp.int32, (1, tk), 1)
    diag_bias = key_bias(n_full) + jnp.where(kpos <= qpos, 0.0, NEG)
    carry = _flash_tile_bias(q3, ks_ref[0, pl.ds(start, tk), :], vs_ref[0, pl.ds(start, tk), :], diag_bias,
                             _flash_init(NSA_G, qb, NSA_DH))
    ocmp_s[...], selb_s[...] = ocmp_next, selb_next
    o_slc = _flash_done(lax.fori_loop(0, n_full, slc_step, carry))

    gt = jax.nn.sigmoid(gt_ref[0, 0])
    parts = []
    for g in range(NSA_G):
        parts.append(gt[:, 3 * g:3 * g + 1] * o_cmp[g] + gt[:, 3 * g + 1:3 * g + 2] * o_slc[g]
                     + gt[:, 3 * g + 2:3 * g + 3] * o_swa[g])
    o_ref[0] = jnp.concatenate(parts, axis=-1)


def nsa_attn(qn, qr, kcmp, vcmp, m_mat, ks, vs, kw, vw, gates, n_slc):
    B, _, ncp, dh = kcmp.shape
    L = qr.shape[1] // B
    nb = m_mat.shape[1]
    qb = Q_BLOCK
    tk = _pick(L, (1024, 512, 256, 128))
    assert L >= WINDOW + qb
    nblk = L // qb
    q_spec = pl.BlockSpec((NSA_G, qb, dh), lambda b, h, i: (h, b * nblk + i, 0))
    q_next = pl.BlockSpec((NSA_G, qb, dh), lambda b, h, i: (h, b * nblk + jnp.minimum(i + 1, nblk - 1), 0))
    cmp_spec = pl.BlockSpec((1, 1, ncp, dh), lambda b, h, i: (b, h, 0, 0))
    kv_spec = pl.BlockSpec((1, L, dh), lambda b, h, i: (h, b, 0))
    return pl.pallas_call(
        functools.partial(_nsa_attn_body, tk=tk, n_slc=n_slc),
        out_shape=jax.ShapeDtypeStruct((B, L, NSA_QW), F32),
        grid=(B, NSA_HKV, nblk),
        in_specs=[q_spec, q_next, q_spec, cmp_spec, cmp_spec, _resident((ncp, nb)), kv_spec, kv_spec, kv_spec, kv_spec,
                  pl.BlockSpec((1, 1, qb, 3 * NSA_G), lambda b, h, i: (b, h, i, 0))],
        out_specs=pl.BlockSpec((1, qb, NSA_G * dh), lambda b, h, i: (b, i, h)),
        scratch_shapes=[pltpu.VMEM((NSA_G, qb, dh), F32), pltpu.VMEM((qb, nb), F32)],
        compiler_params=_cparams(("parallel", "parallel", "arbitrary")),
        name="nsa_attn",
    )(qn, qn, qr, kcmp, vcmp, m_mat, ks, vs, kw, vw, gates)


def _sample_attn_body(pt_ref, qn_ref, qr_ref, kc_ref, vc_ref, kpool_ref, vpool_ref, ksn_ref, vsn_ref,
                      bk_ref, bv_ref, kwn_ref, vwn_ref, gt_ref, m_ref, o_ref,
                      kbuf, vbuf, ksem, vsem, sel_s, m_s, l_s, acc_s, *, pch, L, pos0, n_slc):
    b, ch = pl.program_id(0), pl.program_id(1)
    nb, nch = pl.num_programs(0), pl.num_programs(1)
    step = b * nch + ch
    slot = lax.rem(step, 2)
    R = NSA_G * L
    page = kbuf.shape[4]
    tk = pch * page

    @pl.when(step == 0)
    def _():
        _fetch_pages(pt_ref, kpool_ref, kbuf, ksem, b, ch, slot, pch)
        _fetch_pages(pt_ref, vpool_ref, vbuf, vsem, b, ch, slot, pch)

    @pl.when(step + 1 < nb * nch)
    def _():
        nxt = step + 1
        _fetch_pages(pt_ref, kpool_ref, kbuf, ksem, nxt // nch, lax.rem(nxt, nch), 1 - slot, pch)
        _fetch_pages(pt_ref, vpool_ref, vbuf, vsem, nxt // nch, lax.rem(nxt, nch), 1 - slot, pch)

    row = lax.broadcasted_iota(jnp.int32, (R, 1), 0)
    tq = row & (L - 1)
    qpos = pos0 + tq
    same_t = jnp.where((lax.broadcasted_iota(jnp.int32, (R, R), 0) & (L - 1))
                       == (lax.broadcasted_iota(jnp.int32, (R, R), 1) & (L - 1)), 1.0, 0.0)
    scale = NSA_DH ** -0.5

    @pl.when(ch == 0)
    def _():
        for h in range(NSA_HKV):
            kc = kc_ref[0, h]
            ncp = kc.shape[0]
            s = _dot_nt(qn_ref[0, h] * scale, kc, HIGHEST)
            kend = lax.broadcasted_iota(jnp.int32, (1, ncp), 1) * CMP_STRIDE + (CMP_BLOCK - 1)
            p = _masked_softmax(s, kend <= qpos)
            acc_s[0, h] = _dot(p.astype(BF16), vc_ref[0, h].astype(BF16))
            imp = _dotf(_dotf(same_t, p), m_ref[...])
            sel_s[h] = _select_blocks(imp, qpos, n_slc)
            m_s[h] = jnp.full((R, 1), NEG, F32)
            l_s[h] = jnp.zeros((R, 1), F32)
            acc_s[1, h] = jnp.zeros((R, NSA_DH), F32)

    _wait_pages(kpool_ref, kbuf, ksem, slot, pch)
    _wait_pages(vpool_ref, vbuf, vsem, slot, pch)
    nbl = sel_s.shape[2]
    blk_i = lax.broadcasted_iota(jnp.int32, (nbl, tk), 0)
    key_blk = lax.broadcasted_iota(jnp.int32, (nbl, tk), 1) // SLC_BLOCK + ch * (tk // SLC_BLOCK)
    expand = jnp.where(blk_i == key_blk, 1.0, 0.0).astype(BF16)
    for h in range(NSA_HKV):
        q3 = (qr_ref[0, h] * QK_SCALE_LOG2).astype(BF16)[None]
        mask = _dot(sel_s[h].astype(BF16), expand) > 0.5
        carry = (m_s[h][None], l_s[h][None], acc_s[1, h][None])
        k_t = jnp.concatenate([kbuf[slot, p, h] for p in range(pch)], axis=-1).astype(BF16)
        v_t = jnp.concatenate([vbuf[slot, p, h] for p in range(pch)], axis=-1).astype(BF16)
        m, l, acc = _flash_tile_t(q3, k_t, v_t, jnp.where(mask, 0.0, NEG), carry)
        m_s[h], l_s[h], acc_s[1, h] = m[0], l[0], acc[0]

    @pl.when(ch == nch - 1)
    def _():
        npad = ksn_ref.shape[1]
        tn = lax.broadcasted_iota(jnp.int32, (1, npad), 1)
        new_ok = (tn <= tq) & (tn < L)
        wb = bk_ref.shape[3]
        wpos = pos0 - wb + lax.broadcasted_iota(jnp.int32, (1, wb), 1)
        dw = qpos - wpos
        win_ok = (dw >= 0) & (dw <= WINDOW)
        cur_blk = pos0 // SLC_BLOCK
        for h in range(NSA_HKV):
            sl = slice(h * NSA_DH, (h + 1) * NSA_DH)
            q3 = (qr_ref[0, h] * QK_SCALE_LOG2).astype(BF16)[None]
            sel_new = sel_s[h][:, cur_blk:cur_blk + 1] > 0.5
            carry = (m_s[h][None], l_s[h][None], acc_s[1, h][None])
            carry = _flash_tile(q3, ksn_ref[0][:, sl].astype(BF16), vsn_ref[0][:, sl].astype(BF16),
                                new_ok & sel_new, carry)
            o_slc = _flash_done(carry)[0]
            carry = _flash_init(1, R, NSA_DH)
            carry = _flash_tile_t(q3, bk_ref[0, h].astype(BF16), bv_ref[0, h].astype(BF16),
                                  jnp.where(win_ok, 0.0, NEG), carry)
            carry = _flash_tile(q3, kwn_ref[0][:, sl].astype(BF16), vwn_ref[0][:, sl].astype(BF16), new_ok, carry)
            o_swa = _flash_done(carry)[0]
            gt = jax.nn.sigmoid(gt_ref[0, h])
            o_ref[0, h] = gt[:, 0:1] * acc_s[0, h] + gt[:, 1:2] * o_slc + gt[:, 2:3] * o_swa


def sample_attn(page_table, qn, qr, kcmp, vcmp, kpool, vpool, ks_new, vs_new, buf_k, buf_v, kw_new, vw_new,
                gates, m_mat, L, pos0, n_slc):
    B, n_pages = page_table.shape
    page = kpool.shape[3]
    wb = buf_k.shape[3]
    R = NSA_G * L
    pch = _pick(n_pages, (16, 8, 4, 2, 1))
    nch = n_pages // pch
    nbl = m_mat.shape[1]
    ncp = kcmp.shape[2]
    assert L & (L - 1) == 0 and pos0 % SLC_BLOCK == 0 and pos0 // SLC_BLOCK < nbl
    per_b = lambda *blk: pl.BlockSpec((1,) + blk, lambda b, c, pt: (b,) + (0,) * len(blk))
    return pl.pallas_call(
        functools.partial(_sample_attn_body, pch=pch, L=L, pos0=pos0, n_slc=n_slc),
        out_shape=jax.ShapeDtypeStruct((B, NSA_HKV, R, NSA_DH), F32),
        grid_spec=pltpu.PrefetchScalarGridSpec(
            num_scalar_prefetch=1,
            grid=(B, nch),
            in_specs=[per_b(NSA_HKV, R, NSA_DH), per_b(NSA_HKV, R, NSA_DH),
                      per_b(NSA_HKV, ncp, NSA_DH), per_b(NSA_HKV, ncp, NSA_DH),
                      pl.BlockSpec(memory_space=pl.ANY), pl.BlockSpec(memory_space=pl.ANY),
                      per_b(SUBLANE, NSA_KW), per_b(SUBLANE, NSA_KW),
                      per_b(NSA_HKV, NSA_DH, wb), per_b(NSA_HKV, NSA_DH, wb),
                      per_b(SUBLANE, NSA_KW), per_b(SUBLANE, NSA_KW),
                      per_b(NSA_HKV, R, 3),
                      pl.BlockSpec((ncp, nbl), lambda b, c, pt: (0, 0))],
            out_specs=per_b(NSA_HKV, R, NSA_DH),
            scratch_shapes=[pltpu.VMEM((2, pch, NSA_HKV, NSA_DH, page), F32),
                            pltpu.VMEM((2, pch, NSA_HKV, NSA_DH, page), F32),
                            pltpu.SemaphoreType.DMA((2,)), pltpu.SemaphoreType.DMA((2,)),
                            pltpu.VMEM((NSA_HKV, R, nbl), F32), pltpu.VMEM((NSA_HKV, R, 1), F32),
                            pltpu.VMEM((NSA_HKV, R, 1), F32), pltpu.VMEM((2, NSA_HKV, R, NSA_DH), F32)],
        ),
        compiler_params=_cparams(("arbitrary", "arbitrary")),
        name="sample_attn",
    )(page_table, qn, qr, kcmp, vcmp, kpool, vpool, ks_new, vs_new, buf_k, buf_v, kw_new, vw_new, gates, m_mat)


def _cmp_to_slc(n_cmp_rows, n_cols, n_slc):
    k = np.arange(n_cmp_rows)[:, None] * CMP_STRIDE
    j = np.arange(n_cols)[None, :] * SLC_BLOCK
    ov = np.minimum(k + CMP_BLOCK, j + SLC_BLOCK) - np.maximum(k, j)
    m = np.clip(ov, 0, None).astype(np.float32) / CMP_BLOCK
    m[:, n_slc:] = 0.0
    return jnp.asarray(m)


def _rope_tables(pos0, L):
    half = NSA_DH // 2
    inv = ROPE_THETA ** (-jnp.arange(half, dtype=F32) / half)
    ang = (pos0 + jnp.arange(L)).astype(F32)[:, None] * inv[None, :]
    cos, sin = jnp.cos(ang), jnp.sin(ang)
    return jnp.concatenate([cos, cos], -1), jnp.concatenate([-sin, sin], -1)


def _cmp_weights(w, pe):
    w = w.reshape(2, CMP_STRIDE, NSA_DH, NSA_DH)
    eye = jnp.eye(NSA_HKV, dtype=F32)
    wtaps = jnp.einsum("alde,hg->lhdage", w, eye).reshape(CMP_STRIDE, NSA_KW, 2 * NSA_KW).astype(BF16)
    pe = pe.reshape(2, CMP_STRIDE, 1, NSA_DH)
    pe_rows = jnp.broadcast_to(pe, (2, CMP_STRIDE, NSA_HKV, NSA_DH)).reshape(2, CMP_STRIDE, NSA_KW)
    return wtaps, jnp.pad(pe_rows.transpose(1, 0, 2), ((0, 0), (0, SUBLANE - 2), (0, 0)))


def _nsa_prompt(h, proj, B, L, q_norm, k_norm, pe_k, pe_v, w_ck, w_cv, w_out):
    cos, sin = _rope_tables(0, L)
    qn, qr, ks_hm, vs_hm, kw_hm, vw_hm, kc, vc, ks, vs, kw, vw = nsa_heads(proj, cos, sin, q_norm, k_norm)
    n_seg = L // CMP_STRIDE
    n_slc = L // SLC_BLOCK
    assert L % Q_BLOCK == 0 and n_slc <= LANE and n_slc >= N_SEL
    wk, pek = _cmp_weights(w_ck, pe_k)
    wv, pev = _cmp_weights(w_cv, pe_v)
    kcmp = cmp_finish(cmp_proj(kc, wk), pek, wk, k_norm[0], B, n_seg, True)
    vcmp = cmp_finish(cmp_proj(vc, wv), pev, wv, k_norm[0], B, n_seg, False)
    gate = proj[:, NSA_QW + 6 * NSA_KW:NSA_QW + 6 * NSA_KW + 3 * NSA_HQ]
    gates = gate.reshape(B, L, NSA_HKV, 3 * NSA_G).transpose(0, 2, 1, 3)
    o = nsa_attn(qn, qr, kcmp, vcmp, _cmp_to_slc(n_seg, LANE, n_slc), ks_hm, vs_hm, kw_hm, vw_hm, gates, n_slc)
    n_keep = min(WINDOW, L)
    tok = lambda a: a.reshape(B, L, NSA_HKV, NSA_DH)
    return (mm_res(o.reshape(B * L, NSA_QW), w_out, h),
            (tok(kc), tok(vc), tok(ks), tok(vs), tok(kw)[:, L - n_keep:], tok(vw)[:, L - n_keep:]))


def _nsa_sample(h, proj, B, L, past, page_table, q_norm, k_norm, pe_k, pe_v, w_ck, w_cv, w_out):
    pc_k, pc_v, ps_k, ps_v, buf_k, buf_v = past
    n_pool, page = pc_k.shape[:2]
    n_pages = page_table.shape[1]
    pos0 = n_pages * page
    cos, sin = _rope_tables(pos0, L)
    qn, qr, _, _, _, _, kc, vc, ks, vs, kw, vw = nsa_heads(proj, jnp.tile(cos, (B, 1)), jnp.tile(sin, (B, 1)),
                                                             q_norm, k_norm)
    gate = proj[:, NSA_QW + 6 * NSA_KW:NSA_QW + 6 * NSA_KW + 3 * NSA_HQ]
    assert L <= CMP_STRIDE and page % CMP_STRIDE == 0 and pos0 % SLC_BLOCK == 0
    n_seg = pos0 // CMP_STRIDE
    n_slc = -(-(pos0 + L) // SLC_BLOCK)
    nbl = _round_up(n_slc, LANE)
    wk, pek = _cmp_weights(w_ck, pe_k)
    wv, pev = _cmp_weights(w_cv, pe_v)
    seg_pool = lambda p: p.reshape(n_pool, page, NSA_KW)
    kcmp = cmp_finish(cmp_proj_paged(page_table, seg_pool(pc_k), wk), pek, wk, k_norm[0], B, n_seg, True)
    vcmp = cmp_finish(cmp_proj_paged(page_table, seg_pool(pc_v), wv), pev, wv, k_norm[0], B, n_seg, False)
    rows = lambda a: (a.reshape(NSA_HKV, NSA_G, B, L, NSA_DH).transpose(2, 0, 1, 3, 4)
                      .reshape(B, NSA_HKV, NSA_G * L, NSA_DH))
    keys_t = lambda p: p.transpose(0, 2, 3, 1)
    new8 = lambda a: jnp.pad(a.reshape(B, L, NSA_KW), ((0, 0), (0, SUBLANE - L), (0, 0)))
    tok = lambda a: a.reshape(B, L, NSA_HKV, NSA_DH)
    gates = gate.reshape(B, L, NSA_HKV, NSA_G, 3).transpose(0, 2, 3, 1, 4).reshape(B, NSA_HKV, NSA_G * L, 3)
    o = sample_attn(page_table, rows(qn), rows(qr), kcmp, vcmp, keys_t(ps_k), keys_t(ps_v),
                    new8(ks), new8(vs), keys_t(buf_k), keys_t(buf_v),
                    new8(kw), new8(vw), gates, _cmp_to_slc(n_seg, nbl, n_slc), L, pos0, n_slc)
    o = o.reshape(B, NSA_HKV, NSA_G, L, NSA_DH).transpose(0, 3, 1, 2, 4).reshape(B * L, NSA_QW)
    win_k = jnp.concatenate([buf_k, tok(kw)], axis=1)[:, L:]
    win_v = jnp.concatenate([buf_v, tok(vw)], axis=1)[:, L:]
    return mm_res(o, w_out, h), (tok(kc), tok(vc), tok(ks), tok(vs), win_k, win_v)


def _pad_cols(w, n):
    return jnp.pad(w, ((0, 0), (0, n - w.shape[1]))).astype(BF16)


def kernel(x_prompt, x_sample, state_delta, state_conv, cache_swa_k, cache_swa_v, cache_cmp_k, cache_cmp_v, cache_slc_k, cache_slc_v, page_table, norm_mix, norm_ffn, dn_w_in, dn_conv_w, dn_a_log, dn_dt_bias, dn_norm, dn_w_out, sg_w_in, sg_ln_g, sg_ln_b, sg_w_spatial, sg_b_spatial, sg_w_out, nsa_w_in, nsa_q_norm, nsa_k_norm, nsa_cmp_pe_k, nsa_cmp_pe_v, nsa_cmp_w_k, nsa_cmp_w_v, nsa_w_out, ffn_w_gate, ffn_w_up, ffn_w_down):
    Bp, Lp, D = x_prompt.shape
    Bs, Ls, _ = x_sample.shape
    depth = norm_mix.shape[0]
    hp = x_prompt.reshape(Bp * Lp, D)
    hs = x_sample.reshape(Bs * Ls, D)
    dn_S_p, dn_S_s, dn_c_p, dn_c_s, sg_v_s = [], [], [], [], []
    c_p = ([], [], [], [], [], [])
    c_s = ([], [], [], [], [], [])
    for i in range(depth):
        kind, j = i % N_MIXERS, i // N_MIXERS
        if kind == 0:
            w_in = _pad_cols(dn_w_in[j], _round_up(dn_w_in.shape[2], LANE))
            w_out = dn_w_out[j].astype(BF16)
            prm = (dn_conv_w[j], dn_a_log[j], dn_dt_bias[j], dn_norm[j], w_out)
            proj_p = rms_matmul(hp, norm_mix[i], w_in)
            proj_s = rms_matmul(hs, norm_mix[i], w_in)
            zero_buf = jnp.zeros((Bp, DN_CONV - 1, DN_CONV_DIM), F32)
            zero_S = jnp.zeros((Bp, DN_HV, DN_DK, DN_DV), F32)
            hp, Sp, cp = _delta_group(hp, proj_p, Bp, Lp, zero_buf, zero_S, *prm)
            hs, Ss, cs = _delta_group(hs, proj_s, Bs, Ls, state_conv[j], state_delta[j], *prm)
            dn_S_p.append(Sp); dn_S_s.append(Ss); dn_c_p.append(cp); dn_c_s.append(cs)
        elif kind == 1:
            W = sg_w_in.shape[2] // 2
            w_in = sg_w_in[j].astype(BF16)
            w_out = sg_w_out[j].astype(BF16)
            prm = (sg_ln_g[j], sg_ln_b[j], sg_w_spatial[j], sg_b_spatial[j])
            y_p = rms_matmul(hp, norm_mix[i], w_in, act="gelu")
            y_s = rms_matmul(hs, norm_mix[i], w_in, act="gelu")
            (a_p,) = sg_mix(y_p, *prm, want_v=False)
            y_s = jnp.pad(y_s.reshape(Bs, Ls, 2 * W), ((0, 0), (0, SG_CHUNK - Ls), (0, 0)))
            a_s, v_s = sg_mix(y_s.reshape(Bs * SG_CHUNK, 2 * W), *prm, want_v=True)
            unpad = lambda a: a.reshape(Bs, SG_CHUNK, W)[:, :Ls]
            hp = mm_res(a_p, w_out, hp)
            hs = mm_res(unpad(a_s).reshape(Bs * Ls, W), w_out, hs)
            sg_v_s.append(unpad(v_s))
        else:
            w_in = _pad_cols(nsa_w_in[j], _round_up(nsa_w_in.shape[2], LANE))
            w_out = nsa_w_out[j].astype(BF16)
            prm = (nsa_q_norm[j], nsa_k_norm[j], nsa_cmp_pe_k[j], nsa_cmp_pe_v[j], nsa_cmp_w_k[j], nsa_cmp_w_v[j], w_out)
            past = (cache_cmp_k[j], cache_cmp_v[j], cache_slc_k[j], cache_slc_v[j], cache_swa_k[j], cache_swa_v[j])
            proj_p = rms_matmul(hp, norm_mix[i], w_in)
            proj_s = rms_matmul(hs, norm_mix[i], w_in)
            hp, new_p = _nsa_prompt(hp, proj_p, Bp, Lp, *prm)
            hs, new_s = _nsa_sample(hs, proj_s, Bs, Ls, past, page_table, *prm)
            for lst, a in zip(c_p, new_p):
                lst.append(a)
            for lst, a in zip(c_s, new_s):
                lst.append(a)
        wg, wu, wd = ffn_w_gate[i].astype(BF16), ffn_w_up[i].astype(BF16), ffn_w_down[i].astype(BF16)
        hp = ffn(hp, norm_ffn[i], wg, wu, wd)
        hs = ffn(hs, norm_ffn[i], wg, wu, wd)
    return (hp.reshape(Bp, Lp, D), hs.reshape(Bs, Ls, D),
            jnp.stack(dn_S_p), jnp.stack(dn_S_s), jnp.stack(dn_c_p), jnp.stack(dn_c_s),
            jnp.stack(sg_v_s),
            jnp.stack(c_p[4]), jnp.stack(c_p[5]), jnp.stack(c_s[4]), jnp.stack(c_s[5]),
            jnp.stack(c_p[0]), jnp.stack(c_p[1]), jnp.stack(c_p[2]), jnp.stack(c_p[3]),
            jnp.stack(c_s[0]), jnp.stack(c_s[1]), jnp.stack(c_s[2]), jnp.stack(c_s[3]))
```

```python
import functools
import math

import numpy as np
import jax
import jax.numpy as jnp
from jax import lax
from jax.experimental import pallas as pl
from jax.experimental.pallas import tpu as pltpu

F32 = jnp.float32
BF16 = jnp.bfloat16
HIGHEST = lax.Precision.HIGHEST

N_MIXERS = 3
NORM_EPS = 1e-6
LN_EPS = 1e-5
L2_EPS = 1e-6
DN_DK = 128
DN_DV = 128
DN_HK = 8
DN_HV = 16
DN_CONV = 4
DN_CHUNK = 64
INV_BLOCK = 16
DN_QK = DN_HK * DN_DK
DN_VW = DN_HV * DN_DV
DN_CONV_DIM = 2 * DN_QK + DN_VW
DN_HEADS_ALL = DN_CONV_DIM // DN_DK
SG_CHUNK = 128
SG_GROUPS = 8
NSA_DH = 64
NSA_HQ = 16
NSA_HKV = 4
NSA_G = NSA_HQ // NSA_HKV
CMP_STRIDE = 16
CMP_BLOCK = 32
SLC_BLOCK = 64
N_SEL = 16
WINDOW = 512
Q_BLOCK = 128
FORCE_BONUS = 1e4
ROPE_THETA = 10000.0
NSA_QW = NSA_HQ * NSA_DH
NSA_KW = NSA_HKV * NSA_DH
SEG_W = CMP_STRIDE * NSA_KW
NEG = -1e30
QK_SCALE_LOG2 = NSA_DH ** -0.5 * math.log2(math.e)

LANE = 128
SUBLANE = 8
VMEM_LIMIT = 56 * 1024 * 1024


def _cparams(sem):
    return pltpu.CompilerParams(dimension_semantics=sem, vmem_limit_bytes=VMEM_LIMIT)


def _round_up(n, m):
    return -(-n // m) * m


def _pick(n, cands):
    for c in cands:
        if n % c == 0:
            return c
    raise ValueError(f"no tile for {n} in {cands}")


def _dot(a, b):
    return jnp.dot(a, b, preferred_element_type=F32)


def _dot_nt(a, b, precision=None):
    return lax.dot_general(a, b, (((1,), (1,)), ((), ())), precision=precision, preferred_element_type=F32)


def _dotf(a, b):
    return jnp.dot(a, b, precision=HIGHEST, preferred_element_type=F32)


def _split_bf16(a):
    hi = a.astype(BF16)
    return hi, (a - hi.astype(F32)).astype(BF16)


def _dot3_nt(a, b):
    ah, al = _split_bf16(a)
    bh, bl = _split_bf16(b)
    return _dot_nt(ah, bh) + (_dot_nt(ah, bl) + _dot_nt(al, bh))


def _rms(x, g, eps=NORM_EPS):
    return x * lax.rsqrt(jnp.mean(x * x, -1, keepdims=True) + eps) * g


def _resident(shape):
    nd = len(shape)
    return pl.BlockSpec(shape, lambda *_: (0,) * nd, pipeline_mode=pl.Buffered(1))


def _rms_matmul_body(x_ref, g_ref, w_ref, o_ref, *, cn, act):
    xb = _rms(x_ref[...], g_ref[...]).astype(BF16)
    for c0 in range(0, o_ref.shape[-1], cn):
        y = _dot(xb, w_ref[:, c0:c0 + cn])
        if act == "gelu":
            y = jax.nn.gelu(y)
        o_ref[:, c0:c0 + cn] = y


def rms_matmul(x, g, w, *, act=None):
    T, D = x.shape
    N = w.shape[1]
    tm = _pick(T, (256, 128, 64, 32, 16, 8))
    k = 1
    while not (N % k == 0 and (N // k) % LANE == 0 and N // k <= 1024):
        k += 1
    return pl.pallas_call(
        functools.partial(_rms_matmul_body, cn=N // k, act=act),
        out_shape=jax.ShapeDtypeStruct((T, N), F32),
        grid=(T // tm,),
        in_specs=[pl.BlockSpec((tm, D), lambda i: (i, 0)), _resident((1, D)), _resident((D, N))],
        out_specs=pl.BlockSpec((tm, N), lambda i: (i, 0)),
        compiler_params=_cparams(("parallel",)),
        name="rms_matmul",
    )(x, g.reshape(1, D), w)


def _ffn_body(x_ref, g_ref, wg_ref, wu_ref, wd_ref, o_ref, *, hc):
    x = x_ref[...]
    xb = _rms(x, g_ref[...]).astype(BF16)
    acc = x
    for c0 in range(0, wg_ref.shape[1], hc):
        gate = _dot(xb, wg_ref[:, c0:c0 + hc])
        up = _dot(xb, wu_ref[:, c0:c0 + hc])
        a = (jax.nn.silu(gate) * up).astype(BF16)
        acc = acc + _dot(a, wd_ref[c0:c0 + hc, :])
    o_ref[...] = acc


def ffn(x, g, wg, wu, wd):
    T, D = x.shape
    H = wg.shape[1]
    tm = _pick(T, (512, 256, 128, 64, 32, 16, 8))
    hc = H // 2 if (H // 2) % LANE == 0 else H
    return pl.pallas_call(
        functools.partial(_ffn_body, hc=hc),
        out_shape=jax.ShapeDtypeStruct((T, D), F32),
        grid=(T // tm,),
        in_specs=[pl.BlockSpec((tm, D), lambda i: (i, 0)), _resident((1, D)),
                  _resident((D, H)), _resident((D, H)), _resident((H, D))],
        out_specs=pl.BlockSpec((tm, D), lambda i: (i, 0)),
        compiler_params=_cparams(("parallel",)),
        name="ffn",
    )(x, g.reshape(1, D), wg, wu, wd)


def _mm_res_body(a_ref, w_ref, r_ref, o_ref):
    o_ref[...] = r_ref[...] + _dot(a_ref[...].astype(BF16), w_ref[...])


def mm_res(a, w, res):
    T, K = a.shape
    D = w.shape[1]
    tm = _pick(T, (512, 256, 128, 64, 32, 16, 8))
    return pl.pallas_call(
        _mm_res_body,
        out_shape=jax.ShapeDtypeStruct((T, D), F32),
        grid=(T // tm,),
        in_specs=[pl.BlockSpec((tm, K), lambda i: (i, 0)), _resident((K, D)),
                  pl.BlockSpec((tm, D), lambda i: (i, 0))],
        out_specs=pl.BlockSpec((tm, D), lambda i: (i, 0)),
        compiler_params=_cparams(("parallel",)),
        name="mm_res",
    )(a, w, res)


def _dn_gates_body(p_ref, alog_ref, dtb_ref, beta_ref, g_ref):
    p = p_ref[...]
    beta_ref[...] = jax.nn.sigmoid(p[:, :DN_HV])
    g_ref[...] = -jnp.exp(alog_ref[...]) * jax.nn.softplus(p[:, DN_HV:2 * DN_HV] + dtb_ref[...])


def dn_gates(proj, a_log, dt_bias):
    T = proj.shape[0]
    tm = _pick(T, (512, 256, 128, 64, 32, 16, 8))
    col_blk = (DN_CONV_DIM + DN_VW) // LANE
    return pl.pallas_call(
        _dn_gates_body,
        out_shape=(jax.ShapeDtypeStruct((T, DN_HV), F32),) * 2,
        grid=(T // tm,),
        in_specs=[pl.BlockSpec((tm, LANE), lambda i: (i, col_blk)), _resident((1, DN_HV)), _resident((1, DN_HV))],
        out_specs=(pl.BlockSpec((tm, DN_HV), lambda i: (i, 0)),) * 2,
        compiler_params=_cparams(("parallel",)),
        name="dn_gates",
    )(proj, a_log.reshape(1, DN_HV), dt_bias.reshape(1, DN_HV))


def _dn_conv_body(x_ref, prev_ref, first_ref, w_ref, o_ref, *, nt):
    seq_start = lax.rem(pl.program_id(0), nt) == 0
    for h in range(DN_HEADS_ALL):
        sl = slice(h * DN_DK, (h + 1) * DN_DK)
        x = x_ref[:, sl]
        xc = jnp.concatenate([jnp.where(seq_start, first_ref[0, :, sl], prev_ref[:, sl]), x], axis=0)
        w = w_ref[:, sl]
        acc = x * w[DN_CONV - 1:DN_CONV]
        for k in range(1, DN_CONV):
            shifted = pltpu.roll(xc, k, 0)[SUBLANE:]
            acc = acc + shifted * w[DN_CONV - 1 - k:DN_CONV - k]
        c = jax.nn.silu(acc)
        if h < 2 * DN_HK:
            c = c * lax.rsqrt(jnp.sum(c * c, -1, keepdims=True) + L2_EPS)
        if h < DN_HK:
            c = c * DN_DK ** -0.5
        o_ref[h] = c


def dn_conv(x, first, conv_w, rows, tt, nt):
    per8 = tt // SUBLANE
    return pl.pallas_call(
        functools.partial(_dn_conv_body, nt=nt),
        out_shape=jax.ShapeDtypeStruct((DN_HEADS_ALL, rows, DN_DK), F32),
        grid=(rows // tt,),
        in_specs=[pl.BlockSpec((tt, DN_CONV_DIM), lambda i: (i, 0)),
                  pl.BlockSpec((SUBLANE, DN_CONV_DIM), lambda i: (jnp.maximum(i * per8 - 1, 0), 0)),
                  pl.BlockSpec((1, SUBLANE, DN_CONV_DIM), lambda i: (i // nt, 0, 0)),
                  _resident((DN_CONV, DN_CONV_DIM))],
        out_specs=pl.BlockSpec((DN_HEADS_ALL, tt, DN_DK), lambda i: (0, i, 0)),
        compiler_params=_cparams(("parallel",)),
        name="dn_conv",
    )(x, x, first, conv_w)


def _bdot(a, b):
    return lax.dot_general(a, b, (((2,), (1,)), ((0,), (0,))), preferred_element_type=F32)


def _bdot_nt(a, b):
    return lax.dot_general(a, b, (((2,), (2,)), ((0,), (0,))), preferred_element_type=F32)


def _bdot3(a, b):
    ah, al = _split_bf16(a)
    bh, bl = _split_bf16(b)
    return _bdot(ah, bh) + (_bdot(ah, bl) + _bdot(al, bh))


def _delta_body(q_ref, k_ref, v_ref, g_ref, b_ref, s0_ref, o_ref, sout_ref, S_ref, *, hb):
    c = pl.program_id(1)
    C = DN_CHUNK

    @pl.when(c == 0)
    def _():
        S_ref[...] = s0_ref[0]

    ri = lax.broadcasted_iota(jnp.int32, (C, C), 0)
    ci = lax.broadcasted_iota(jnp.int32, (C, C), 1)
    tri = ri >= ci
    stri = ri > ci
    eye = jnp.where(ri == ci, 1.0, 0.0).astype(F32)
    bi, bj = ri // INV_BLOCK, ci // INV_BLOCK
    rep = DN_HV // DN_HK
    nq = hb // rep
    q = q_ref[...][:, None]
    k = k_ref[...][:, None]
    kb16 = k_ref[...].astype(BF16)
    kk = _bdot_nt(kb16, kb16)[:, None]
    qk0 = _bdot_nt(q_ref[...].astype(BF16), kb16)[:, None]
    grow = g_ref[...].reshape(nq, rep, 1, C)
    brow = b_ref[...].reshape(nq, rep, 1, C)
    gcol = jnp.sum(eye * grow, -1, keepdims=True)
    bcol = jnp.sum(eye * brow, -1, keepdims=True)
    gc_col = jnp.sum(jnp.where(tri, grow, 0.0), -1, keepdims=True)
    gc_row = jnp.sum(jnp.where(ri <= ci, gcol, 0.0), -2, keepdims=True)
    decay = jnp.exp(jnp.where(tri, gc_col - gc_row, NEG))
    a_mat = (jnp.where(stri, kk * decay, 0.0) * bcol).reshape(hb, C, C)
    eg = jnp.exp(gc_col)
    v = v_ref[...].reshape(nq, rep, C, DN_DV)
    rhs = jnp.concatenate([v * bcol, k * (bcol * eg)], axis=-1).reshape(hb, C, 2 * DN_DV)
    p = -jnp.where(bi == bj, a_mat, 0.0)
    t = eye + p
    for _ in range(int(math.log2(INV_BLOCK)) - 1):
        p = _bdot3(p, p)
        t = t + _bdot3(t, p)
    width = INV_BLOCK
    while width < C:
        e = jnp.where((bi // (2 * width // INV_BLOCK) == bj // (2 * width // INV_BLOCK))
                      & (bi // (width // INV_BLOCK) != bj // (width // INV_BLOCK)), a_mat, 0.0)
        tb = t.astype(BF16)
        t = t - _bdot(_bdot(tb, e.astype(BF16)).astype(BF16), tb)
        width *= 2
    sol = _bdot(t.astype(BF16), rhs.astype(BF16))
    g_last = gc_col[:, :, C - 1:C]
    u = sol[:, :, :DN_DV]
    wq = jnp.concatenate([sol[:, :, DN_DV:], (q * eg).reshape(hb, C, DN_DK)], axis=1).astype(BF16)
    qk = jnp.where(tri, qk0 * decay, 0.0).reshape(hb, C, C).astype(BF16)
    kdec_t = jnp.swapaxes((k * jnp.exp(g_last - gc_col)).reshape(hb, C, DN_DK), 1, 2).astype(BF16)
    g_keep = jnp.exp(g_last).reshape(hb, 1, 1)

    S = S_ref[...]
    ws = _bdot(wq, S.astype(BF16))
    vb = (u - ws[:, :C]).astype(BF16)
    o_ref[...] = ws[:, C:] + _bdot(qk, vb)
    S_ref[...] = S * g_keep + _bdot(kdec_t, vb)

    @pl.when(c == pl.num_programs(1) - 1)
    def _():
        sout_ref[0] = S_ref[...]


def delta_rule(qkv_hm, g_rows, b_rows, s0, B, n):
    hb = DN_HV
    C = DN_CHUNK
    state_blk = pl.BlockSpec((1, hb, DN_DK, DN_DV), lambda b, c: (b, 0, 0, 0))
    gate_blk = pl.BlockSpec((hb, 1, 1, C), lambda b, c: (0, b * n + c, 0, 0))
    return pl.pallas_call(
        functools.partial(_delta_body, hb=hb),
        out_shape=(jax.ShapeDtypeStruct((DN_HV, B * n * C, DN_DV), F32),
                   jax.ShapeDtypeStruct((B, DN_HV, DN_DK, DN_DV), F32)),
        grid=(B, n),
        in_specs=[
            pl.BlockSpec((DN_HK, C, DN_DK), lambda b, c: (0, b * n + c, 0)),
            pl.BlockSpec((DN_HK, C, DN_DK), lambda b, c: (1, b * n + c, 0)),
            pl.BlockSpec((DN_HV, C, DN_DV), lambda b, c: (1, b * n + c, 0)),
            gate_blk, gate_blk, state_blk,
        ],
        out_specs=(pl.BlockSpec((hb, C, DN_DV), lambda b, c: (0, b * n + c, 0)), state_blk),
        scratch_shapes=[pltpu.VMEM((hb, DN_DK, DN_DV), F32)],
        compiler_params=_cparams(("parallel", "arbitrary")),
        name="delta_rule",
    )(qkv_hm, qkv_hm, qkv_hm, g_rows, b_rows, s0)


def _dn_out_body(o_ref, z_ref, g_ref, w_ref, r_ref, out_ref):
    parts = []
    for h in range(DN_HV):
        z = z_ref[:, h * DN_DV:(h + 1) * DN_DV]
        parts.append((_rms(o_ref[h], g_ref[...]) * jax.nn.silu(z)).astype(BF16))
    out_ref[...] = r_ref[...] + _dot(jnp.concatenate(parts, axis=-1), w_ref[...])


def dn_out(o_hm, proj, norm_g, w_out, res):
    T, D = res.shape
    tm = _pick(T, (256, 128, 64, 32, 16, 8))
    return pl.pallas_call(
        _dn_out_body,
        out_shape=jax.ShapeDtypeStruct((T, D), F32),
        grid=(T // tm,),
        in_specs=[pl.BlockSpec((DN_HV, tm, DN_DV), lambda i: (0, i, 0)),
                  pl.BlockSpec((tm, DN_VW), lambda i: (i, DN_CONV_DIM // DN_VW)),
                  _resident((1, DN_DV)), _resident((DN_VW, D)),
                  pl.BlockSpec((tm, D), lambda i: (i, 0))],
        out_specs=pl.BlockSpec((tm, D), lambda i: (i, 0)),
        compiler_params=_cparams(("parallel",)),
        name="dn_out",
    )(o_hm, proj, norm_g.reshape(1, DN_DV), w_out, res)


def _delta_group(h, proj, B, L, conv_buf, s0, conv_w, a_log, dt_bias, norm_g, w_out):
    C = DN_CHUNK
    Lp = _round_up(L, C)
    n = Lp // C
    tail = proj.reshape(B, L, proj.shape[1])[:, max(L - (DN_CONV - 1), 0):, :DN_CONV_DIM]
    new_buf = jnp.concatenate([conv_buf, tail], axis=1)[:, -(DN_CONV - 1):]
    tt = _pick(Lp, (256, 128, 64))
    nt = Lp // tt
    if Lp == L:
        x = proj
    else:
        qkv = proj[:, :DN_CONV_DIM].reshape(B, L, DN_CONV_DIM)
        x = jnp.pad(qkv, ((0, 0), (0, Lp - L), (0, 0))).reshape(B * Lp, DN_CONV_DIM)
    first = jnp.pad(conv_buf, ((0, 0), (SUBLANE - (DN_CONV - 1), 0), (0, 0)))
    qkv_hm = dn_conv(x, first, conv_w, B * Lp, tt, nt)
    beta, g = dn_gates(proj, a_log, dt_bias)

    def rows(a):
        a = jnp.pad(a.reshape(B, L, DN_HV), ((0, 0), (0, Lp - L), (0, 0)))
        return a.reshape(B, n, C, DN_HV).transpose(3, 0, 1, 2).reshape(DN_HV, B * n, 1, C)

    o_hm, S = delta_rule(qkv_hm, rows(g), rows(beta), s0, B, n)
    if Lp != L:
        o_hm = o_hm.reshape(DN_HV, B, Lp, DN_DV)[:, :, :L].reshape(DN_HV, B * L, DN_DV)
    return dn_out(o_hm, proj, norm_g, w_out, h), S, new_buf


def _sg_body(y_ref, lng_ref, lnb_ref, wsp_ref, bsp_ref, a_ref, *v_out, width):
    u = y_ref[:, :width]
    v = y_ref[:, width:]
    xc = v - jnp.mean(v, -1, keepdims=True)
    vn = xc * lax.rsqrt(jnp.mean(xc * xc, -1, keepdims=True) + LN_EPS) * lng_ref[...] + lnb_ref[...]
    if v_out:
        v_out[0][...] = vn
    n = wsp_ref.shape[1]
    gw = width // SG_GROUPS
    tril = lax.broadcasted_iota(jnp.int32, (n, n), 0) >= lax.broadcasted_iota(jnp.int32, (n, n), 1)
    for g in range(SG_GROUPS):
        sl = slice(g * gw, (g + 1) * gw)
        wt = jnp.where(tril, wsp_ref[g], 0.0).astype(BF16)
        mixed = _dot(wt, vn[:, sl].astype(BF16)) + bsp_ref[:, g:g + 1]
        a_ref[:, sl] = u[:, sl] * mixed


def sg_mix(y, ln_g, ln_b, w_sp, b_sp, want_v):
    T = y.shape[0]
    W = y.shape[1] // 2
    n = SG_CHUNK
    out_shape = [jax.ShapeDtypeStruct((T, W), F32)]
    out_specs = [pl.BlockSpec((n, W), lambda i: (i, 0))]
    if want_v:
        out_shape.append(jax.ShapeDtypeStruct((T, W), F32))
        out_specs.append(pl.BlockSpec((n, W), lambda i: (i, 0)))
    return pl.pallas_call(
        functools.partial(_sg_body, width=W),
        out_shape=tuple(out_shape),
        grid=(T // n,),
        in_specs=[pl.BlockSpec((n, 2 * W), lambda i: (i, 0)), _resident((1, W)), _resident((1, W)),
                  _resident((SG_GROUPS, n, n)), _resident((n, SG_GROUPS))],
        out_specs=tuple(out_specs),
        compiler_params=_cparams(("parallel",)),
        name="sg_mix",
    )(y, ln_g.reshape(1, W), ln_b.reshape(1, W), w_sp, b_sp.T)


def _nsa_heads_body(p_ref, cos_ref, sin_ref, qg_ref, kg_ref, qn_ref, qr_ref, ksh_ref, vsh_ref, kwh_ref, vwh_ref,
                    kc_ref, vc_ref, kst_ref, vst_ref, kwt_ref, vwt_ref):
    cos, sin = cos_ref[...], sin_ref[...]
    half = NSA_DH // 2

    def head(col0, h):
        return p_ref[:, col0 + h * NSA_DH:col0 + (h + 1) * NSA_DH]

    def rope(xn):
        return xn * cos + jnp.concatenate([xn[:, half:], xn[:, :half]], axis=-1) * sin

    for h in range(NSA_HQ):
        xn = _rms(head(0, h), qg_ref[...])
        qn_ref[h] = xn
        qr_ref[h] = rope(xn)
    col = NSA_QW
    kc_ref[...] = p_ref[:, col:col + NSA_KW]
    vc_ref[...] = p_ref[:, col + NSA_KW:col + 2 * NSA_KW]
    col += 2 * NSA_KW
    for kind, (kh_ref, kt_ref, vh_ref, vt_ref) in enumerate(((ksh_ref, kst_ref, vsh_ref, vst_ref),
                                                             (kwh_ref, kwt_ref, vwh_ref, vwt_ref))):
        parts = []
        for h in range(NSA_HKV):
            r = rope(_rms(head(col, h), kg_ref[kind + 1:kind + 2]))
            kh_ref[h] = r.astype(BF16)
            parts.append(r)
            vh_ref[h] = head(col + NSA_KW, h).astype(BF16)
        kt_ref[...] = jnp.concatenate(parts, axis=-1)
        vt_ref[...] = p_ref[:, col + NSA_KW:col + 2 * NSA_KW]
        col += 2 * NSA_KW


def nsa_heads(proj, cos, sin, q_norm, k_norm):
    T, N = proj.shape
    Tc = cos.shape[0]
    tl = _pick(Tc, (256, 128, 64, 32, 16, 8))
    n_tab = Tc // tl
    tab = pl.BlockSpec((tl, NSA_DH), lambda i: (i % n_tab, 0))
    hm = lambda H: pl.BlockSpec((H, tl, NSA_DH), lambda i: (0, i, 0))
    tok = pl.BlockSpec((tl, NSA_KW), lambda i: (i, 0))
    return pl.pallas_call(
        _nsa_heads_body,
        out_shape=((jax.ShapeDtypeStruct((NSA_HQ, T, NSA_DH), F32),) * 2
                   + (jax.ShapeDtypeStruct((NSA_HKV, T, NSA_DH), BF16),) * 4
                   + (jax.ShapeDtypeStruct((T, NSA_KW), F32),) * 6),
        grid=(T // tl,),
        in_specs=[pl.BlockSpec((tl, N), lambda i: (i, 0)), tab, tab, _resident((1, NSA_DH)), _resident((3, NSA_DH))],
        out_specs=(hm(NSA_HQ),) * 2 + (hm(NSA_HKV),) * 4 + (tok,) * 6,
        compiler_params=_cparams(("parallel",)),
        name="nsa_heads",
    )(proj, cos, sin, q_norm.reshape(1, NSA_DH), k_norm)


def _cmp_proj_body(*refs):
    x_refs, w_ref, o_ref = refs[:-2], refs[-2], refs[-1]
    n = o_ref.shape[0]

    def tap(l):
        x = jnp.concatenate([r[pl.ds(l, n, stride=CMP_STRIDE), :] for r in x_refs], axis=-1)
        return _dot(x.astype(BF16), w_ref[l])

    acc = tap(0)
    for l in range(1, CMP_STRIDE):
        acc = acc + tap(l)
    o_ref[...] = acc


def cmp_proj(x, wtaps):
    rows = x.shape[0]
    segs = rows // CMP_STRIDE
    tm = _pick(segs, (256, 128, 64, 32, 16, 8))
    return pl.pallas_call(
        _cmp_proj_body,
        out_shape=jax.ShapeDtypeStruct((segs, 2 * NSA_KW), F32),
        grid=(segs // tm,),
        in_specs=[pl.BlockSpec((tm * CMP_STRIDE, LANE), lambda i, c=c: (i, c)) for c in range(NSA_KW // LANE)]
                 + [_resident((CMP_STRIDE, NSA_KW, 2 * NSA_KW))],
        out_specs=pl.BlockSpec((tm, 2 * NSA_KW), lambda i: (i, 0)),
        compiler_params=_cparams(("parallel",)),
        name="cmp_proj",
    )(*([x] * (NSA_KW // LANE)), wtaps)


def _page_copies(pool_ref, buf_ref, sem_ref, page, slot, p):
    if buf_ref.ndim == pool_ref.ndim + 1:
        return [pltpu.make_async_copy(pool_ref.at[page], buf_ref.at[slot, p], sem_ref.at[slot])]
    return [pltpu.make_async_copy(pool_ref.at[page, :, pl.ds(t * LANE, LANE)], buf_ref.at[t, slot, p], sem_ref.at[slot])
            for t in range(buf_ref.shape[0])]


def _fetch_pages(pt_ref, pool_ref, buf_ref, sem_ref, b, chunk, slot, pch):
    for p in range(pch):
        for cp in _page_copies(pool_ref, buf_ref, sem_ref, pt_ref[b, chunk * pch + p], slot, p):
            cp.start()


def _wait_pages(pool_ref, buf_ref, sem_ref, slot, pch):
    for p in range(pch):
        for cp in _page_copies(pool_ref, buf_ref, sem_ref, 0, slot, p):
            cp.wait()


def _cmp_proj_paged_body(pt_ref, pool_ref, w_ref, o_ref, buf_ref, sem_ref, tok_ref, *, pch):
    b, ch = pl.program_id(0), pl.program_id(1)
    nb, nch = pl.num_programs(0), pl.num_programs(1)
    step = b * nch + ch
    slot = lax.rem(step, 2)

    @pl.when(step == 0)
    def _():
        _fetch_pages(pt_ref, pool_ref, buf_ref, sem_ref, b, ch, slot, pch)

    @pl.when(step + 1 < nb * nch)
    def _():
        nxt = step + 1
        _fetch_pages(pt_ref, pool_ref, buf_ref, sem_ref, nxt // nch, lax.rem(nxt, nch), 1 - slot, pch)

    _wait_pages(pool_ref, buf_ref, sem_ref, slot, pch)
    tiles, _, page, _ = tok_ref.shape
    segs = page // CMP_STRIDE
    for p in range(pch):
        rows = buf_ref[slot, p].reshape(NSA_KW, page).T
        for t in range(tiles):
            tok_ref[t, p] = rows[:, t * LANE:(t + 1) * LANE]

    def tap(l):
        x = jnp.concatenate([tok_ref[t, :, pl.ds(l, segs, stride=CMP_STRIDE), :] for t in range(tiles)],
                            axis=-1)
        return _dot(x.reshape(pch * segs, NSA_KW).astype(BF16), w_ref[l])

    acc = tap(0)
    for l in range(1, CMP_STRIDE):
        acc = acc + tap(l)
    o_ref[...] = acc


def cmp_proj_paged(page_table, pool, wtaps):
    B, n_pages = page_table.shape
    page = pool.shape[3]
    segs = page // CMP_STRIDE
    pch = _pick(n_pages, (32, 16, 8, 4, 2, 1))
    nch = n_pages // pch
    return pl.pallas_call(
        functools.partial(_cmp_proj_paged_body, pch=pch),
        out_shape=jax.ShapeDtypeStruct((B * n_pages * segs, 2 * NSA_KW), F32),
        grid_spec=pltpu.PrefetchScalarGridSpec(
            num_scalar_prefetch=1,
            grid=(B, nch),
            in_specs=[pl.BlockSpec(memory_space=pl.ANY),
                      pl.BlockSpec((CMP_STRIDE, NSA_KW, 2 * NSA_KW), lambda b, c, pt: (0, 0, 0))],
            out_specs=pl.BlockSpec((pch * segs, 2 * NSA_KW), lambda b, c, pt: (b * nch + c, 0)),
            scratch_shapes=[pltpu.VMEM((2, pch, NSA_HKV, NSA_DH, page), F32), pltpu.SemaphoreType.DMA((2,)),
                            pltpu.VMEM((NSA_KW // LANE, pch, page, LANE), F32)],
        ),
        compiler_params=_cparams(("arbitrary", "arbitrary")),
        name="cmp_proj_paged",
    )(page_table, pool, wtaps)


def _cmp_finish_body(a_ref, pe_ref, w_ref, g_ref, o_ref, *, do_norm):
    a = a_ref[0]
    n = a.shape[0]
    pe = _dot(pe_ref[0].astype(BF16), w_ref[0])
    for l in range(1, CMP_STRIDE):
        pe = pe + _dot(pe_ref[l].astype(BF16), w_ref[l])
    pe_term = pe[0:1, :NSA_KW] + pe[1:2, NSA_KW:]
    blk = a[:, :NSA_KW] + pltpu.roll(a[:, NSA_KW:], n - 1, 0) + pe_term
    for h in range(NSA_HKV):
        bh = blk[:, h * NSA_DH:(h + 1) * NSA_DH]
        if do_norm:
            bh = _rms(bh, g_ref[...])
        o_ref[0, h] = bh


def cmp_finish(a, pe_rows, wtaps, g, B, n, do_norm):
    return pl.pallas_call(
        functools.partial(_cmp_finish_body, do_norm=do_norm),
        out_shape=jax.ShapeDtypeStruct((B, NSA_HKV, n, NSA_DH), F32),
        grid=(B,),
        in_specs=[pl.BlockSpec((1, n, 2 * NSA_KW), lambda b: (b, 0, 0)), _resident((CMP_STRIDE, SUBLANE, NSA_KW)),
                  _resident((CMP_STRIDE, NSA_KW, 2 * NSA_KW)), _resident((1, NSA_DH))],
        out_specs=pl.BlockSpec((1, NSA_HKV, n, NSA_DH), lambda b: (b, 0, 0, 0)),
        compiler_params=_cparams(("parallel",)),
        name="cmp_finish",
    )(a.reshape(B, n, 2 * NSA_KW), pe_rows, wtaps, g.reshape(1, NSA_DH))


def _masked_softmax(s, mask):
    sm = jnp.where(mask, s, NEG)
    m = jnp.max(sm, -1, keepdims=True)
    p = jnp.where(mask, jnp.exp(s - m), 0.0)
    d = jnp.sum(p, -1, keepdims=True)
    return p / jnp.where(d > 0, d, 1.0)


def _topk_rounds(work, sel, k):
    nl = work.shape[-1]
    lane = lax.broadcasted_iota(jnp.int32, work.shape, 1).astype(F32)
    for _ in range(k):
        m = jnp.max(work, -1, keepdims=True)
        idx = jnp.min(jnp.where(work == m, lane, float(nl)), -1, keepdims=True)
        hit = lane == idx
        sel = jnp.where(hit, 1.0, sel)
        work = jnp.where(hit, -jnp.inf, work)
    return work, sel


SEL_ROUNDS = N_SEL - 3


def _select_start(imp, qpos, n_lanes_valid):
    blk = lax.broadcasted_iota(jnp.int32, imp.shape, 1)
    cur = qpos // SLC_BLOCK
    eligible = (blk * SLC_BLOCK <= qpos) & (blk < n_lanes_valid)
    forced = (blk == 0) | (blk == cur) | (blk == cur - 1)
    assert NSA_G < FORCE_BONUS
    return jnp.where(eligible & ~forced, imp, -jnp.inf), jnp.where(eligible & forced, 1.0, 0.0)


def _select_blocks(imp, qpos, n_lanes_valid):
    return _topk_rounds(*_select_start(imp, qpos, n_lanes_valid), SEL_ROUNDS)[1]


def _flash_tile(q3, k, v, mask, carry):
    return _flash_tile_bias(q3, k, v, jnp.where(mask, 0.0, NEG), carry)


def _flash_tile_bias(q3, k, v, bias, carry):
    m, l, acc = carry
    G, R, dh = q3.shape
    tk = k.shape[0]
    s = _dot_nt(q3.reshape(G * R, dh), k).reshape(G, R, tk) + bias[None]
    m_new = jnp.maximum(m, jnp.max(s, -1, keepdims=True))
    alpha = jnp.exp2(m - m_new)
    p = jnp.exp2(s - m_new)
    l = alpha * l + jnp.sum(p, -1, keepdims=True)
    pv = _dot(p.reshape(G * R, tk).astype(BF16), v).reshape(G, R, dh)
    return m_new, l, alpha * acc + pv


def _flash_tile_t(q3, k_t, v_t, bias, carry):
    m, l, acc = carry
    G, R, dh = q3.shape
    tk = k_t.shape[1]
    s = _dot(q3.reshape(G * R, dh), k_t).reshape(G, R, tk) + bias[None]
    m_new = jnp.maximum(m, jnp.max(s, -1, keepdims=True))
    alpha = jnp.exp2(m - m_new)
    p = jnp.exp2(s - m_new)
    l = alpha * l + jnp.sum(p, -1, keepdims=True)
    pv = _dot_nt(p.reshape(G * R, tk).astype(BF16), v_t).reshape(G, R, dh)
    return m_new, l, alpha * acc + pv


def _flash_init(G, R, dh):
    return (jnp.full((G, R, 1), NEG, F32), jnp.zeros((G, R, 1), F32), jnp.zeros((G, R, dh), F32))


def _flash_done(carry):
    m, l, acc = carry
    return jnp.where(m > 0.5 * NEG, acc / l, 0.0)


def _cmp_branch(qn, kc, vc, overlap, qpos, n_slc):
    G, qb, dh = qn.shape
    ncp = kc.shape[0]
    s = _dot3_nt(qn.reshape(G * qb, dh) * NSA_DH ** -0.5, kc).reshape(G, qb, ncp)
    kend = lax.broadcasted_iota(jnp.int32, (1, ncp), 1) * CMP_STRIDE + (CMP_BLOCK - 1)
    p = _masked_softmax(s, (kend <= qpos)[None])
    o_cmp = _dot(p.reshape(G * qb, ncp).astype(BF16), vc.astype(BF16)).reshape(G, qb, dh)
    psum = p[0]
    for g in range(1, G):
        psum = psum + p[g]
    ps_hi, ps_lo = _split_bf16(psum)
    sel = _select_blocks(_dot(ps_hi, overlap) + _dot(ps_lo, overlap), qpos, n_slc)
    return o_cmp, jnp.where(sel > 0.5, 0.0, NEG)


def _nsa_attn_body(qn_ref, qnn_ref, q_ref, kc_ref, vc_ref, m_ref, ks_ref, vs_ref, kw_ref, vw_ref, gt_ref, o_ref,
                   ocmp_s, selb_s, *, tk, n_slc):
    i = pl.program_id(2)
    qb = q_ref.shape[1]
    nb = m_ref.shape[1]
    row = lax.broadcasted_iota(jnp.int32, (qb, 1), 0)
    qpos = i * qb + row
    kc, vc = kc_ref[0, 0], vc_ref[0, 0]
    overlap = m_ref[...].astype(BF16)

    @pl.when(i == 0)
    def _():
        ocmp_s[...], selb_s[...] = _cmp_branch(qn_ref[...], kc, vc, overlap, row, n_slc)

    o_cmp, sel_bias = ocmp_s[...], selb_s[...]
    ocmp_next, selb_next = _cmp_branch(qnn_ref[...], kc, vc, overlap, qpos + qb, n_slc)

    q3 = (q_ref[...] * QK_SCALE_LOG2).astype(BF16)
    slab = WINDOW + qb
    start = pl.multiple_of(jnp.maximum(i * qb - WINDOW, 0), qb)
    d = qpos - (start + lax.broadcasted_iota(jnp.int32, (1, slab), 1))
    o_swa = _flash_done(_flash_tile(q3, kw_ref[0, pl.ds(start, slab), :], vw_ref[0, pl.ds(start, slab), :],
                                    (d >= 0) & (d <= WINDOW), _flash_init(NSA_G, qb, NSA_DH)))

    bpt = tk // SLC_BLOCK
    expand = jnp.where(lax.broadcasted_iota(jnp.int32, (nb, tk), 0)
                       == lax.broadcasted_iota(jnp.int32, (nb, tk), 1) // SLC_BLOCK, 1.0, 0.0).astype(BF16)

    def key_bias(j):
        rolled = pltpu.roll(sel_bias, lax.rem(nb - j * bpt, nb), 1)
        return _dot(rolled.astype(BF16), expand)

    def slc_step(j, carry):
        start = pl.multiple_of(j * tk, tk)
        return _flash_tile_bias(q3, ks_ref[0, pl.ds(start, tk), :], vs_ref[0, pl.ds(start, tk), :],
                                key_bias(j), carry)

    assert tk % qb == 0
    n_full = (i * qb) // tk
    start = pl.multiple_of(n_full * tk, tk)
    kpos = n_full * tk + lax.broadcasted_iota(jnp.int32, (1, tk), 1)
    diag_bias = key_bias(n_full) + jnp.where(kpos <= qpos, 0.0, NEG)
    carry = _flash_tile_bias(q3, ks_ref[0, pl.ds(start, tk), :], vs_ref[0, pl.ds(start, tk), :], diag_bias,
                             _flash_init(NSA_G, qb, NSA_DH))
    ocmp_s[...], selb_s[...] = ocmp_next, selb_next
    o_slc = _flash_done(lax.fori_loop(0, n_full, slc_step, carry))

    gt = jax.nn.sigmoid(gt_ref[0, 0])
    parts = []
    for g in range(NSA_G):
        parts.append(gt[:, 3 * g:3 * g + 1] * o_cmp[g] + gt[:, 3 * g + 1:3 * g + 2] * o_slc[g]
                     + gt[:, 3 * g + 2:3 * g + 3] * o_swa[g])
    o_ref[0] = jnp.concatenate(parts, axis=-1)


def nsa_attn(qn, qr, kcmp, vcmp, m_mat, ks, vs, kw, vw, gates, n_slc):
    B, _, ncp, dh = kcmp.shape
    L = qr.shape[1] // B
    nb = m_mat.shape[1]
    qb = Q_BLOCK
    tk = _pick(L, (1024, 512, 256, 128))
    assert L >= WINDOW + qb
    nblk = L // qb
    q_spec = pl.BlockSpec((NSA_G, qb, dh), lambda b, h, i: (h, b * nblk + i, 0))
    q_next = pl.BlockSpec((NSA_G, qb, dh), lambda b, h, i: (h, b * nblk + jnp.minimum(i + 1, nblk - 1), 0))
    cmp_spec = pl.BlockSpec((1, 1, ncp, dh), lambda b, h, i: (b, h, 0, 0))
    kv_spec = pl.BlockSpec((1, L, dh), lambda b, h, i: (h, b, 0))
    return pl.pallas_call(
        functools.partial(_nsa_attn_body, tk=tk, n_slc=n_slc),
        out_shape=jax.ShapeDtypeStruct((B, L, NSA_QW), F32),
        grid=(B, NSA_HKV, nblk),
        in_specs=[q_spec, q_next, q_spec, cmp_spec, cmp_spec, _resident((ncp, nb)), kv_spec, kv_spec, kv_spec, kv_spec,
                  pl.BlockSpec((1, 1, qb, 3 * NSA_G), lambda b, h, i: (b, h, i, 0))],
        out_specs=pl.BlockSpec((1, qb, NSA_G * dh), lambda b, h, i: (b, i, h)),
        scratch_shapes=[pltpu.VMEM((NSA_G, qb, dh), F32), pltpu.VMEM((qb, nb), F32)],
        compiler_params=_cparams(("parallel", "parallel", "arbitrary")),
        name="nsa_attn",
    )(qn, qn, qr, kcmp, vcmp, m_mat, ks, vs, kw, vw, gates)


def _sample_attn_body(pt_ref, qn_ref, qr_ref, kc_ref, vc_ref, kpool_ref, vpool_ref, ksn_ref, vsn_ref,
                      bk_ref, bv_ref, kwn_ref, vwn_ref, gt_ref, m_ref, o_ref,
                      kbuf, vbuf, ksem, vsem, sel_s, m_s, l_s, acc_s, *, pch, L, pos0, n_slc):
    b, ch = pl.program_id(0), pl.program_id(1)
    nb, nch = pl.num_programs(0), pl.num_programs(1)
    step = b * nch + ch
    slot = lax.rem(step, 2)
    R = NSA_G * L
    page = kbuf.shape[4]
    tk = pch * page

    @pl.when(step == 0)
    def _():
        _fetch_pages(pt_ref, kpool_ref, kbuf, ksem, b, ch, slot, pch)
        _fetch_pages(pt_ref, vpool_ref, vbuf, vsem, b, ch, slot, pch)

    @pl.when(step + 1 < nb * nch)
    def _():
        nxt = step + 1
        _fetch_pages(pt_ref, kpool_ref, kbuf, ksem, nxt // nch, lax.rem(nxt, nch), 1 - slot, pch)
        _fetch_pages(pt_ref, vpool_ref, vbuf, vsem, nxt // nch, lax.rem(nxt, nch), 1 - slot, pch)

    row = lax.broadcasted_iota(jnp.int32, (R, 1), 0)
    tq = row & (L - 1)
    qpos = pos0 + tq
    same_t = jnp.where((lax.broadcasted_iota(jnp.int32, (R, R), 0) & (L - 1))
                       == (lax.broadcasted_iota(jnp.int32, (R, R), 1) & (L - 1)), 1.0, 0.0)
    scale = NSA_DH ** -0.5

    @pl.when(ch == 0)
    def _():
        for h in range(NSA_HKV):
            kc = kc_ref[0, h]
            ncp = kc.shape[0]
            s = _dot_nt(qn_ref[0, h] * scale, kc, HIGHEST)
            kend = lax.broadcasted_iota(jnp.int32, (1, ncp), 1) * CMP_STRIDE + (CMP_BLOCK - 1)
            p = _masked_softmax(s, kend <= qpos)
            acc_s[0, h] = _dot(p.astype(BF16), vc_ref[0, h].astype(BF16))
            imp = _dotf(_dotf(same_t, p), m_ref[...])
            sel_s[h] = _select_blocks(imp, qpos, n_slc)
            m_s[h] = jnp.full((R, 1), NEG, F32)
            l_s[h] = jnp.zeros((R, 1), F32)
            acc_s[1, h] = jnp.zeros((R, NSA_DH), F32)

    _wait_pages(kpool_ref, kbuf, ksem, slot, pch)
    _wait_pages(vpool_ref, vbuf, vsem, slot, pch)
    nbl = sel_s.shape[2]
    bpc = tk // SLC_BLOCK
    assert bpc <= LANE and nbl % LANE == 0
    expand = jnp.where(lax.broadcasted_iota(jnp.int32, (LANE, tk), 0)
                       == lax.broadcasted_iota(jnp.int32, (LANE, tk), 1) // SLC_BLOCK, 1.0, 0.0).astype(BF16)
    sel_bias = jnp.where(sel_s[...].reshape(NSA_HKV * R, nbl) > 0.5, 0.0, NEG)
    rolled = pltpu.roll(sel_bias, lax.rem(nbl - ch * bpc, nbl), 1)[:, :LANE]
    key_bias = _dot(rolled.astype(BF16), expand)
    for h in range(NSA_HKV):
        q3 = (qr_ref[0, h] * QK_SCALE_LOG2).astype(BF16)[None]
        carry = (m_s[h][None], l_s[h][None], acc_s[1, h][None])
        k_t = jnp.concatenate([kbuf[slot, p, h] for p in range(pch)], axis=-1).astype(BF16)
        v_t = jnp.concatenate([vbuf[slot, p, h] for p in range(pch)], axis=-1).astype(BF16)
        m, l, acc = _flash_tile_t(q3, k_t, v_t, key_bias[h * R:(h + 1) * R], carry)
        m_s[h], l_s[h], acc_s[1, h] = m[0], l[0], acc[0]

    @pl.when(ch == nch - 1)
    def _():
        npad = ksn_ref.shape[1]
        tn = lax.broadcasted_iota(jnp.int32, (1, npad), 1)
        new_ok = (tn <= tq) & (tn < L)
        wb = bk_ref.shape[3]
        wpos = pos0 - wb + lax.broadcasted_iota(jnp.int32, (1, wb), 1)
        dw = qpos - wpos
        win_ok = (dw >= 0) & (dw <= WINDOW)
        cur_blk = pos0 // SLC_BLOCK
        for h in range(NSA_HKV):
            sl = slice(h * NSA_DH, (h + 1) * NSA_DH)
            q3 = (qr_ref[0, h] * QK_SCALE_LOG2).astype(BF16)[None]
            sel_new = sel_s[h][:, cur_blk:cur_blk + 1] > 0.5
            carry = (m_s[h][None], l_s[h][None], acc_s[1, h][None])
            carry = _flash_tile(q3, ksn_ref[0][:, sl].astype(BF16), vsn_ref[0][:, sl].astype(BF16),
                                new_ok & sel_new, carry)
            o_slc = _flash_done(carry)[0]
            carry = _flash_init(1, R, NSA_DH)
            carry = _flash_tile_t(q3, bk_ref[0, h].astype(BF16), bv_ref[0, h].astype(BF16),
                                  jnp.where(win_ok, 0.0, NEG), carry)
            carry = _flash_tile(q3, kwn_ref[0][:, sl].astype(BF16), vwn_ref[0][:, sl].astype(BF16), new_ok, carry)
            o_swa = _flash_done(carry)[0]
            gt = jax.nn.sigmoid(gt_ref[0, h])
            o_ref[0, h] = gt[:, 0:1] * acc_s[0, h] + gt[:, 1:2] * o_slc + gt[:, 2:3] * o_swa


def sample_attn(page_table, qn, qr, kcmp, vcmp, kpool, vpool, ks_new, vs_new, buf_k, buf_v, kw_new, vw_new,
                gates, m_mat, L, pos0, n_slc):
    B, n_pages = page_table.shape
    page = kpool.shape[3]
    wb = buf_k.shape[3]
    R = NSA_G * L
    pch = _pick(n_pages, (16, 8, 4, 2, 1))
    nch = n_pages // pch
    nbl = m_mat.shape[1]
    ncp = kcmp.shape[2]
    assert L & (L - 1) == 0 and pos0 % SLC_BLOCK == 0 and pos0 // SLC_BLOCK < nbl
    per_b = lambda *blk: pl.BlockSpec((1,) + blk, lambda b, c, pt: (b,) + (0,) * len(blk))
    return pl.pallas_call(
        functools.partial(_sample_attn_body, pch=pch, L=L, pos0=pos0, n_slc=n_slc),
        out_shape=jax.ShapeDtypeStruct((B, NSA_HKV, R, NSA_DH), F32),
        grid_spec=pltpu.PrefetchScalarGridSpec(
            num_scalar_prefetch=1,
            grid=(B, nch),
            in_specs=[per_b(NSA_HKV, R, NSA_DH), per_b(NSA_HKV, R, NSA_DH),
                      per_b(NSA_HKV, ncp, NSA_DH), per_b(NSA_HKV, ncp, NSA_DH),
                      pl.BlockSpec(memory_space=pl.ANY), pl.BlockSpec(memory_space=pl.ANY),
                      per_b(SUBLANE, NSA_KW), per_b(SUBLANE, NSA_KW),
                      per_b(NSA_HKV, NSA_DH, wb), per_b(NSA_HKV, NSA_DH, wb),
                      per_b(SUBLANE, NSA_KW), per_b(SUBLANE, NSA_KW),
                      per_b(NSA_HKV, R, 3),
                      pl.BlockSpec((ncp, nbl), lambda b, c, pt: (0, 0))],
            out_specs=per_b(NSA_HKV, R, NSA_DH),
            scratch_shapes=[pltpu.VMEM((2, pch, NSA_HKV, NSA_DH, page), F32),
                            pltpu.VMEM((2, pch, NSA_HKV, NSA_DH, page), F32),
                            pltpu.SemaphoreType.DMA((2,)), pltpu.SemaphoreType.DMA((2,)),
                            pltpu.VMEM((NSA_HKV, R, nbl), F32), pltpu.VMEM((NSA_HKV, R, 1), F32),
                            pltpu.VMEM((NSA_HKV, R, 1), F32), pltpu.VMEM((2, NSA_HKV, R, NSA_DH), F32)],
        ),
        compiler_params=_cparams(("arbitrary", "arbitrary")),
        name="sample_attn",
    )(page_table, qn, qr, kcmp, vcmp, kpool, vpool, ks_new, vs_new, buf_k, buf_v, kw_new, vw_new, gates, m_mat)


def _cmp_to_slc(n_cmp_rows, n_cols, n_slc):
    k = np.arange(n_cmp_rows)[:, None] * CMP_STRIDE
    j = np.arange(n_cols)[None, :] * SLC_BLOCK
    ov = np.minimum(k + CMP_BLOCK, j + SLC_BLOCK) - np.maximum(k, j)
    m = np.clip(ov, 0, None).astype(np.float32) / CMP_BLOCK
    m[:, n_slc:] = 0.0
    return jnp.asarray(m)


def _rope_tables(pos0, L):
    half = NSA_DH // 2
    inv = ROPE_THETA ** (-jnp.arange(half, dtype=F32) / half)
    ang = (pos0 + jnp.arange(L)).astype(F32)[:, None] * inv[None, :]
    cos, sin = jnp.cos(ang), jnp.sin(ang)
    return jnp.concatenate([cos, cos], -1), jnp.concatenate([-sin, sin], -1)


def _cmp_weights(w, pe):
    w = w.reshape(2, CMP_STRIDE, NSA_DH, NSA_DH)
    eye = jnp.eye(NSA_HKV, dtype=F32)
    wtaps = jnp.einsum("alde,hg->lhdage", w, eye).reshape(CMP_STRIDE, NSA_KW, 2 * NSA_KW).astype(BF16)
    pe = pe.reshape(2, CMP_STRIDE, 1, NSA_DH)
    pe_rows = jnp.broadcast_to(pe, (2, CMP_STRIDE, NSA_HKV, NSA_DH)).reshape(2, CMP_STRIDE, NSA_KW)
    return wtaps, jnp.pad(pe_rows.transpose(1, 0, 2), ((0, 0), (0, SUBLANE - 2), (0, 0)))


def _nsa_prompt(h, proj, B, L, q_norm, k_norm, pe_k, pe_v, w_ck, w_cv, w_out):
    cos, sin = _rope_tables(0, L)
    qn, qr, ks_hm, vs_hm, kw_hm, vw_hm, kc, vc, ks, vs, kw, vw = nsa_heads(proj, cos, sin, q_norm, k_norm)
    n_seg = L // CMP_STRIDE
    n_slc = L // SLC_BLOCK
    assert L % Q_BLOCK == 0 and n_slc <= LANE and n_slc >= N_SEL
    wk, pek = _cmp_weights(w_ck, pe_k)
    wv, pev = _cmp_weights(w_cv, pe_v)
    kcmp = cmp_finish(cmp_proj(kc, wk), pek, wk, k_norm[0], B, n_seg, True)
    vcmp = cmp_finish(cmp_proj(vc, wv), pev, wv, k_norm[0], B, n_seg, False)
    gate = proj[:, NSA_QW + 6 * NSA_KW:NSA_QW + 6 * NSA_KW + 3 * NSA_HQ]
    gates = gate.reshape(B, L, NSA_HKV, 3 * NSA_G).transpose(0, 2, 1, 3)
    o = nsa_attn(qn, qr, kcmp, vcmp, _cmp_to_slc(n_seg, LANE, n_slc), ks_hm, vs_hm, kw_hm, vw_hm, gates, n_slc)
    n_keep = min(WINDOW, L)
    tok = lambda a: a.reshape(B, L, NSA_HKV, NSA_DH)
    return (mm_res(o.reshape(B * L, NSA_QW), w_out, h),
            (tok(kc), tok(vc), tok(ks), tok(vs), tok(kw)[:, L - n_keep:], tok(vw)[:, L - n_keep:]))


def _nsa_sample(h, proj, B, L, past, page_table, q_norm, k_norm, pe_k, pe_v, w_ck, w_cv, w_out):
    pc_k, pc_v, ps_k, ps_v, buf_k, buf_v = past
    n_pool, page = pc_k.shape[:2]
    n_pages = page_table.shape[1]
    pos0 = n_pages * page
    cos, sin = _rope_tables(pos0, L)
    qn, qr, _, _, _, _, kc, vc, ks, vs, kw, vw = nsa_heads(proj, jnp.tile(cos, (B, 1)), jnp.tile(sin, (B, 1)),
                                                             q_norm, k_norm)
    gate = proj[:, NSA_QW + 6 * NSA_KW:NSA_QW + 6 * NSA_KW + 3 * NSA_HQ]
    assert L <= CMP_STRIDE and page % CMP_STRIDE == 0 and pos0 % SLC_BLOCK == 0
    n_seg = pos0 // CMP_STRIDE
    n_slc = -(-(pos0 + L) // SLC_BLOCK)
    nbl = _round_up(n_slc, LANE)
    wk, pek = _cmp_weights(w_ck, pe_k)
    wv, pev = _cmp_weights(w_cv, pe_v)
    keys_t = lambda p: p.transpose(0, 2, 3, 1)
    kcmp = cmp_finish(cmp_proj_paged(page_table, keys_t(pc_k), wk), pek, wk, k_norm[0], B, n_seg, True)
    vcmp = cmp_finish(cmp_proj_paged(page_table, keys_t(pc_v), wv), pev, wv, k_norm[0], B, n_seg, False)
    rows = lambda a: (a.reshape(NSA_HKV, NSA_G, B, L, NSA_DH).transpose(2, 0, 1, 3, 4)
                      .reshape(B, NSA_HKV, NSA_G * L, NSA_DH))
    new8 = lambda a: jnp.pad(a.reshape(B, L, NSA_KW), ((0, 0), (0, SUBLANE - L), (0, 0)))
    tok = lambda a: a.reshape(B, L, NSA_HKV, NSA_DH)
    gates = gate.reshape(B, L, NSA_HKV, NSA_G, 3).transpose(0, 2, 3, 1, 4).reshape(B, NSA_HKV, NSA_G * L, 3)
    o = sample_attn(page_table, rows(qn), rows(qr), kcmp, vcmp, keys_t(ps_k), keys_t(ps_v),
                    new8(ks), new8(vs), keys_t(buf_k), keys_t(buf_v),
                    new8(kw), new8(vw), gates, _cmp_to_slc(n_seg, nbl, n_slc), L, pos0, n_slc)
    o = o.reshape(B, NSA_HKV, NSA_G, L, NSA_DH).transpose(0, 3, 1, 2, 4).reshape(B * L, NSA_QW)
    win_k = jnp.concatenate([buf_k, tok(kw)], axis=1)[:, L:]
    win_v = jnp.concatenate([buf_v, tok(vw)], axis=1)[:, L:]
    return mm_res(o, w_out, h), (tok(kc), tok(vc), tok(ks), tok(vs), win_k, win_v)


def _pad_cols(w, n):
    return jnp.pad(w, ((0, 0), (0, n - w.shape[1]))).astype(BF16)


def kernel(x_prompt, x_sample, state_delta, state_conv, cache_swa_k, cache_swa_v, cache_cmp_k, cache_cmp_v, cache_slc_k, cache_slc_v, page_table, norm_mix, norm_ffn, dn_w_in, dn_conv_w, dn_a_log, dn_dt_bias, dn_norm, dn_w_out, sg_w_in, sg_ln_g, sg_ln_b, sg_w_spatial, sg_b_spatial, sg_w_out, nsa_w_in, nsa_q_norm, nsa_k_norm, nsa_cmp_pe_k, nsa_cmp_pe_v, nsa_cmp_w_k, nsa_cmp_w_v, nsa_w_out, ffn_w_gate, ffn_w_up, ffn_w_down):
    Bp, Lp, D = x_prompt.shape
    Bs, Ls, _ = x_sample.shape
    depth = norm_mix.shape[0]
    hp = x_prompt.reshape(Bp * Lp, D)
    hs = x_sample.reshape(Bs * Ls, D)
    dn_S_p, dn_S_s, dn_c_p, dn_c_s, sg_v_s = [], [], [], [], []
    c_p = ([], [], [], [], [], [])
    c_s = ([], [], [], [], [], [])
    for i in range(depth):
        kind, j = i % N_MIXERS, i // N_MIXERS
        if kind == 0:
            w_in = _pad_cols(dn_w_in[j], _round_up(dn_w_in.shape[2], LANE))
            w_out = dn_w_out[j].astype(BF16)
            prm = (dn_conv_w[j], dn_a_log[j], dn_dt_bias[j], dn_norm[j], w_out)
            proj_p = rms_matmul(hp, norm_mix[i], w_in)
            proj_s = rms_matmul(hs, norm_mix[i], w_in)
            zero_buf = jnp.zeros((Bp, DN_CONV - 1, DN_CONV_DIM), F32)
            zero_S = jnp.zeros((Bp, DN_HV, DN_DK, DN_DV), F32)
            hp, Sp, cp = _delta_group(hp, proj_p, Bp, Lp, zero_buf, zero_S, *prm)
            hs, Ss, cs = _delta_group(hs, proj_s, Bs, Ls, state_conv[j], state_delta[j], *prm)
            dn_S_p.append(Sp); dn_S_s.append(Ss); dn_c_p.append(cp); dn_c_s.append(cs)
        elif kind == 1:
            W = sg_w_in.shape[2] // 2
            w_in = sg_w_in[j].astype(BF16)
            w_out = sg_w_out[j].astype(BF16)
            prm = (sg_ln_g[j], sg_ln_b[j], sg_w_spatial[j], sg_b_spatial[j])
            y_p = rms_matmul(hp, norm_mix[i], w_in, act="gelu")
            y_s = rms_matmul(hs, norm_mix[i], w_in, act="gelu")
            (a_p,) = sg_mix(y_p, *prm, want_v=False)
            y_s = jnp.pad(y_s.reshape(Bs, Ls, 2 * W), ((0, 0), (0, SG_CHUNK - Ls), (0, 0)))
            a_s, v_s = sg_mix(y_s.reshape(Bs * SG_CHUNK, 2 * W), *prm, want_v=True)
            unpad = lambda a: a.reshape(Bs, SG_CHUNK, W)[:, :Ls]
            hp = mm_res(a_p, w_out, hp)
            hs = mm_res(unpad(a_s).reshape(Bs * Ls, W), w_out, hs)
            sg_v_s.append(unpad(v_s))
        else:
            w_in = _pad_cols(nsa_w_in[j], _round_up(nsa_w_in.shape[2], LANE))
            w_out = nsa_w_out[j].astype(BF16)
            prm = (nsa_q_norm[j], nsa_k_norm[j], nsa_cmp_pe_k[j], nsa_cmp_pe_v[j], nsa_cmp_w_k[j], nsa_cmp_w_v[j], w_out)
            past = (cache_cmp_k[j], cache_cmp_v[j], cache_slc_k[j], cache_slc_v[j], cache_swa_k[j], cache_swa_v[j])
            proj_p = rms_matmul(hp, norm_mix[i], w_in)
            proj_s = rms_matmul(hs, norm_mix[i], w_in)
            hp, new_p = _nsa_prompt(hp, proj_p, Bp, Lp, *prm)
            hs, new_s = _nsa_sample(hs, proj_s, Bs, Ls, past, page_table, *prm)
            for lst, a in zip(c_p, new_p):
                lst.append(a)
            for lst, a in zip(c_s, new_s):
                lst.append(a)
        wg, wu, wd = ffn_w_gate[i].astype(BF16), ffn_w_up[i].astype(BF16), ffn_w_down[i].astype(BF16)
        hp = ffn(hp, norm_ffn[i], wg, wu, wd)
        hs = ffn(hs, norm_ffn[i], wg, wu, wd)
    return (hp.reshape(Bp, Lp, D), hs.reshape(Bs, Ls, D),
            jnp.stack(dn_S_p), jnp.stack(dn_S_s), jnp.stack(dn_c_p), jnp.stack(dn_c_s),
            jnp.stack(sg_v_s),
            jnp.stack(c_p[4]), jnp.stack(c_p[5]), jnp.stack(c_s[4]), jnp.stack(c_s[5]),
            jnp.stack(c_p[0]), jnp.stack(c_p[1]), jnp.stack(c_p[2]), jnp.stack(c_p[3]),
            jnp.stack(c_s[0]), jnp.stack(c_s[1]), jnp.stack(c_s[2]), jnp.stack(c_s[3]))
```

```python
import functools
import math

import numpy as np
import jax
import jax.numpy as jnp
from jax import lax
from jax.experimental import pallas as pl
from jax.experimental.pallas import tpu as pltpu

F32 = jnp.float32
BF16 = jnp.bfloat16
HIGHEST = lax.Precision.HIGHEST

N_MIXERS = 3
NORM_EPS = 1e-6
LN_EPS = 1e-5
L2_EPS = 1e-6
DN_DK = 128
DN_DV = 128
DN_HK = 8
DN_HV = 16
DN_CONV = 4
DN_CHUNK = 64
INV_BLOCK = 8
DN_QK = DN_HK * DN_DK
DN_VW = DN_HV * DN_DV
DN_CONV_DIM = 2 * DN_QK + DN_VW
DN_HEADS_ALL = DN_CONV_DIM // DN_DK
SG_CHUNK = 128
SG_GROUPS = 8
NSA_DH = 64
NSA_HQ = 16
NSA_HKV = 4
NSA_G = NSA_HQ // NSA_HKV
CMP_STRIDE = 16
CMP_BLOCK = 32
SLC_BLOCK = 64
N_SEL = 16
WINDOW = 512
Q_BLOCK = 128
FORCE_BONUS = 1e4
ROPE_THETA = 10000.0
NSA_QW = NSA_HQ * NSA_DH
NSA_KW = NSA_HKV * NSA_DH
SEG_W = CMP_STRIDE * NSA_KW
NEG = -1e30
QK_SCALE_LOG2 = NSA_DH ** -0.5 * math.log2(math.e)

LANE = 128
SUBLANE = 8
VMEM_LIMIT = 56 * 1024 * 1024


def _cparams(sem):
    return pltpu.CompilerParams(dimension_semantics=sem, vmem_limit_bytes=VMEM_LIMIT)


def _round_up(n, m):
    return -(-n // m) * m


def _pick(n, cands):
    for c in cands:
        if n % c == 0:
            return c
    raise ValueError(f"no tile for {n} in {cands}")


def _dot(a, b):
    return jnp.dot(a, b, preferred_element_type=F32)


def _dot_nt(a, b, precision=None):
    return lax.dot_general(a, b, (((1,), (1,)), ((), ())), precision=precision, preferred_element_type=F32)


def _dotf(a, b):
    return jnp.dot(a, b, precision=HIGHEST, preferred_element_type=F32)


def _split_bf16(a):
    hi = a.astype(BF16)
    return hi, (a - hi.astype(F32)).astype(BF16)


def _dot3_nt(a, b):
    ah, al = _split_bf16(a)
    bh, bl = _split_bf16(b)
    return _dot_nt(ah, bh) + (_dot_nt(ah, bl) + _dot_nt(al, bh))


def _rms(x, g, eps=NORM_EPS):
    return x * lax.rsqrt(jnp.mean(x * x, -1, keepdims=True) + eps) * g


def _resident(shape):
    nd = len(shape)
    return pl.BlockSpec(shape, lambda *_: (0,) * nd, pipeline_mode=pl.Buffered(1))


def _rms_matmul_body(x_ref, g_ref, w_ref, o_ref, *, cn, act):
    xb = _rms(x_ref[...], g_ref[...]).astype(BF16)
    for c0 in range(0, o_ref.shape[-1], cn):
        y = _dot(xb, w_ref[:, c0:c0 + cn])
        if act == "gelu":
            y = jax.nn.gelu(y)
        o_ref[:, c0:c0 + cn] = y


def rms_matmul(x, g, w, *, act=None):
    T, D = x.shape
    N = w.shape[1]
    tm = _pick(T, (256, 128, 64, 32, 16, 8))
    k = 1
    while not (N % k == 0 and (N // k) % LANE == 0 and N // k <= 1024):
        k += 1
    return pl.pallas_call(
        functools.partial(_rms_matmul_body, cn=N // k, act=act),
        out_shape=jax.ShapeDtypeStruct((T, N), F32),
        grid=(T // tm,),
        in_specs=[pl.BlockSpec((tm, D), lambda i: (i, 0)), _resident((1, D)), _resident((D, N))],
        out_specs=pl.BlockSpec((tm, N), lambda i: (i, 0)),
        compiler_params=_cparams(("parallel",)),
        name="rms_matmul",
    )(x, g.reshape(1, D), w)


def _ffn_body(x_ref, g_ref, wg_ref, wu_ref, wd_ref, o_ref, *, hc):
    x = x_ref[...]
    xb = _rms(x, g_ref[...]).astype(BF16)
    acc = x
    for c0 in range(0, wg_ref.shape[1], hc):
        gate = _dot(xb, wg_ref[:, c0:c0 + hc])
        up = _dot(xb, wu_ref[:, c0:c0 + hc])
        a = (jax.nn.silu(gate) * up).astype(BF16)
        acc = acc + _dot(a, wd_ref[c0:c0 + hc, :])
    o_ref[...] = acc


def ffn(x, g, wg, wu, wd):
    T, D = x.shape
    H = wg.shape[1]
    tm = _pick(T, (512, 256, 128, 64, 32, 16, 8))
    hc = H // 2 if (H // 2) % LANE == 0 else H
    return pl.pallas_call(
        functools.partial(_ffn_body, hc=hc),
        out_shape=jax.ShapeDtypeStruct((T, D), F32),
        grid=(T // tm,),
        in_specs=[pl.BlockSpec((tm, D), lambda i: (i, 0)), _resident((1, D)),
                  _resident((D, H)), _resident((D, H)), _resident((H, D))],
        out_specs=pl.BlockSpec((tm, D), lambda i: (i, 0)),
        compiler_params=_cparams(("parallel",)),
        name="ffn",
    )(x, g.reshape(1, D), wg, wu, wd)


def _mm_res_body(a_ref, w_ref, r_ref, o_ref):
    o_ref[...] = r_ref[...] + _dot(a_ref[...].astype(BF16), w_ref[...])


def mm_res(a, w, res):
    T, K = a.shape
    D = w.shape[1]
    tm = _pick(T, (512, 256, 128, 64, 32, 16, 8))
    return pl.pallas_call(
        _mm_res_body,
        out_shape=jax.ShapeDtypeStruct((T, D), F32),
        grid=(T // tm,),
        in_specs=[pl.BlockSpec((tm, K), lambda i: (i, 0)), _resident((K, D)),
                  pl.BlockSpec((tm, D), lambda i: (i, 0))],
        out_specs=pl.BlockSpec((tm, D), lambda i: (i, 0)),
        compiler_params=_cparams(("parallel",)),
        name="mm_res",
    )(a, w, res)


def _dn_gates_body(p_ref, alog_ref, dtb_ref, beta_ref, g_ref):
    p = p_ref[...]
    beta_ref[...] = jax.nn.sigmoid(p[:, :DN_HV])
    g_ref[...] = -jnp.exp(alog_ref[...]) * jax.nn.softplus(p[:, DN_HV:2 * DN_HV] + dtb_ref[...])


def dn_gates(proj, a_log, dt_bias):
    T = proj.shape[0]
    tm = _pick(T, (512, 256, 128, 64, 32, 16, 8))
    col_blk = (DN_CONV_DIM + DN_VW) // LANE
    return pl.pallas_call(
        _dn_gates_body,
        out_shape=(jax.ShapeDtypeStruct((T, DN_HV), F32),) * 2,
        grid=(T // tm,),
        in_specs=[pl.BlockSpec((tm, LANE), lambda i: (i, col_blk)), _resident((1, DN_HV)), _resident((1, DN_HV))],
        out_specs=(pl.BlockSpec((tm, DN_HV), lambda i: (i, 0)),) * 2,
        compiler_params=_cparams(("parallel",)),
        name="dn_gates",
    )(proj, a_log.reshape(1, DN_HV), dt_bias.reshape(1, DN_HV))


def _dn_conv_body(x_ref, prev_ref, first_ref, w_ref, o_ref, *, nt):
    seq_start = lax.rem(pl.program_id(0), nt) == 0
    for h in range(DN_HEADS_ALL):
        sl = slice(h * DN_DK, (h + 1) * DN_DK)
        x = x_ref[:, sl]
        xc = jnp.concatenate([jnp.where(seq_start, first_ref[0, :, sl], prev_ref[:, sl]), x], axis=0)
        w = w_ref[:, sl]
        acc = x * w[DN_CONV - 1:DN_CONV]
        for k in range(1, DN_CONV):
            shifted = pltpu.roll(xc, k, 0)[SUBLANE:]
            acc = acc + shifted * w[DN_CONV - 1 - k:DN_CONV - k]
        c = jax.nn.silu(acc)
        if h < 2 * DN_HK:
            c = c * lax.rsqrt(jnp.sum(c * c, -1, keepdims=True) + L2_EPS)
        if h < DN_HK:
            c = c * DN_DK ** -0.5
        o_ref[h] = c


def dn_conv(x, first, conv_w, rows, tt, nt):
    per8 = tt // SUBLANE
    return pl.pallas_call(
        functools.partial(_dn_conv_body, nt=nt),
        out_shape=jax.ShapeDtypeStruct((DN_HEADS_ALL, rows, DN_DK), F32),
        grid=(rows // tt,),
        in_specs=[pl.BlockSpec((tt, DN_CONV_DIM), lambda i: (i, 0)),
                  pl.BlockSpec((SUBLANE, DN_CONV_DIM), lambda i: (jnp.maximum(i * per8 - 1, 0), 0)),
                  pl.BlockSpec((1, SUBLANE, DN_CONV_DIM), lambda i: (i // nt, 0, 0)),
                  _resident((DN_CONV, DN_CONV_DIM))],
        out_specs=pl.BlockSpec((DN_HEADS_ALL, tt, DN_DK), lambda i: (0, i, 0)),
        compiler_params=_cparams(("parallel",)),
        name="dn_conv",
    )(x, x, first, conv_w)


def _bdot(a, b):
    return lax.dot_general(a, b, (((2,), (1,)), ((0,), (0,))), preferred_element_type=F32)


def _bdot_nt(a, b):
    return lax.dot_general(a, b, (((2,), (2,)), ((0,), (0,))), preferred_element_type=F32)


def _delta_body(q_ref, k_ref, v_ref, g_ref, b_ref, s0_ref, o_ref, sout_ref, S_ref, *, hb):
    c = pl.program_id(1)
    C = DN_CHUNK

    @pl.when(c == 0)
    def _():
        S_ref[...] = s0_ref[0]

    ri = lax.broadcasted_iota(jnp.int32, (C, C), 0)
    ci = lax.broadcasted_iota(jnp.int32, (C, C), 1)
    tri = ri >= ci
    stri = ri > ci
    eye = jnp.where(ri == ci, 1.0, 0.0).astype(F32)
    bi, bj = ri // INV_BLOCK, ci // INV_BLOCK
    rep = DN_HV // DN_HK
    nq = hb // rep
    q = q_ref[...][:, None]
    k = k_ref[...][:, None]
    kb16 = k_ref[...].astype(BF16)
    kk = _bdot_nt(kb16, kb16)[:, None]
    qk0 = _bdot_nt(q_ref[...].astype(BF16), kb16)[:, None]
    grow = g_ref[...].reshape(nq, rep, 1, C)
    brow = b_ref[...].reshape(nq, rep, 1, C)
    gcol = jnp.sum(eye * grow, -1, keepdims=True)
    bcol = jnp.sum(eye * brow, -1, keepdims=True)
    gc_col = jnp.sum(jnp.where(tri, grow, 0.0), -1, keepdims=True)
    gc_row = jnp.sum(jnp.where(ri <= ci, gcol, 0.0), -2, keepdims=True)
    decay = jnp.exp(jnp.where(tri, gc_col - gc_row, NEG))
    a_mat = (jnp.where(stri, kk * decay, 0.0) * bcol).reshape(hb, C, C)
    eg = jnp.exp(gc_col)
    v = v_ref[...].reshape(nq, rep, C, DN_DV)
    rhs = jnp.concatenate([v * bcol, k * (bcol * eg)], axis=-1).reshape(hb, C, 2 * DN_DV)
    p = -jnp.where(bi == bj, a_mat, 0.0)
    t = eye + p
    for _ in range(int(math.log2(INV_BLOCK)) - 1):
        pb = p.astype(BF16)
        p = _bdot(pb, pb)
        t = t + _bdot(t.astype(BF16), p.astype(BF16))
    width = INV_BLOCK
    while width < C:
        e = jnp.where((bi // (2 * width // INV_BLOCK) == bj // (2 * width // INV_BLOCK))
                      & (bi // (width // INV_BLOCK) != bj // (width // INV_BLOCK)), a_mat, 0.0)
        tb = t.astype(BF16)
        t = t - _bdot(_bdot(tb, e.astype(BF16)).astype(BF16), tb)
        width *= 2
    sol = _bdot(t.astype(BF16), rhs.astype(BF16))
    g_last = gc_col[:, :, C - 1:C]
    u = sol[:, :, :DN_DV]
    wq = jnp.concatenate([sol[:, :, DN_DV:], (q * eg).reshape(hb, C, DN_DK)], axis=1).astype(BF16)
    qk = jnp.where(tri, qk0 * decay, 0.0).reshape(hb, C, C).astype(BF16)
    kdec_t = jnp.swapaxes((k * jnp.exp(g_last - gc_col)).reshape(hb, C, DN_DK), 1, 2).astype(BF16)
    g_keep = jnp.exp(g_last).reshape(hb, 1, 1)

    S = S_ref[...]
    ws = _bdot(wq, S.astype(BF16))
    vb = (u - ws[:, :C]).astype(BF16)
    o_ref[...] = ws[:, C:] + _bdot(qk, vb)
    S_ref[...] = S * g_keep + _bdot(kdec_t, vb)

    @pl.when(c == pl.num_programs(1) - 1)
    def _():
        sout_ref[0] = S_ref[...]


def delta_rule(qkv_hm, g_rows, b_rows, s0, B, n):
    hb = DN_HV
    C = DN_CHUNK
    state_blk = pl.BlockSpec((1, hb, DN_DK, DN_DV), lambda b, c: (b, 0, 0, 0))
    gate_blk = pl.BlockSpec((hb, 1, 1, C), lambda b, c: (0, b * n + c, 0, 0))
    return pl.pallas_call(
        functools.partial(_delta_body, hb=hb),
        out_shape=(jax.ShapeDtypeStruct((DN_HV, B * n * C, DN_DV), F32),
                   jax.ShapeDtypeStruct((B, DN_HV, DN_DK, DN_DV), F32)),
        grid=(B, n),
        in_specs=[
            pl.BlockSpec((DN_HK, C, DN_DK), lambda b, c: (0, b * n + c, 0)),
            pl.BlockSpec((DN_HK, C, DN_DK), lambda b, c: (1, b * n + c, 0)),
            pl.BlockSpec((DN_HV, C, DN_DV), lambda b, c: (1, b * n + c, 0)),
            gate_blk, gate_blk, state_blk,
        ],
        out_specs=(pl.BlockSpec((hb, C, DN_DV), lambda b, c: (0, b * n + c, 0)), state_blk),
        scratch_shapes=[pltpu.VMEM((hb, DN_DK, DN_DV), F32)],
        compiler_params=_cparams(("parallel", "arbitrary")),
        name="delta_rule",
    )(qkv_hm, qkv_hm, qkv_hm, g_rows, b_rows, s0)


def _dn_out_body(o_ref, z_ref, g_ref, w_ref, r_ref, out_ref):
    parts = []
    for h in range(DN_HV):
        z = z_ref[:, h * DN_DV:(h + 1) * DN_DV]
        parts.append((_rms(o_ref[h], g_ref[...]) * jax.nn.silu(z)).astype(BF16))
    out_ref[...] = r_ref[...] + _dot(jnp.concatenate(parts, axis=-1), w_ref[...])


def dn_out(o_hm, proj, norm_g, w_out, res):
    T, D = res.shape
    tm = _pick(T, (256, 128, 64, 32, 16, 8))
    return pl.pallas_call(
        _dn_out_body,
        out_shape=jax.ShapeDtypeStruct((T, D), F32),
        grid=(T // tm,),
        in_specs=[pl.BlockSpec((DN_HV, tm, DN_DV), lambda i: (0, i, 0)),
                  pl.BlockSpec((tm, DN_VW), lambda i: (i, DN_CONV_DIM // DN_VW)),
                  _resident((1, DN_DV)), _resident((DN_VW, D)),
                  pl.BlockSpec((tm, D), lambda i: (i, 0))],
        out_specs=pl.BlockSpec((tm, D), lambda i: (i, 0)),
        compiler_params=_cparams(("parallel",)),
        name="dn_out",
    )(o_hm, proj, norm_g.reshape(1, DN_DV), w_out, res)


def _delta_group(h, proj, B, L, conv_buf, s0, conv_w, a_log, dt_bias, norm_g, w_out):
    C = DN_CHUNK
    Lp = _round_up(L, C)
    n = Lp // C
    tail = proj.reshape(B, L, proj.shape[1])[:, max(L - (DN_CONV - 1), 0):, :DN_CONV_DIM]
    new_buf = jnp.concatenate([conv_buf, tail], axis=1)[:, -(DN_CONV - 1):]
    tt = _pick(Lp, (256, 128, 64))
    nt = Lp // tt
    if Lp == L:
        x = proj
    else:
        qkv = proj[:, :DN_CONV_DIM].reshape(B, L, DN_CONV_DIM)
        x = jnp.pad(qkv, ((0, 0), (0, Lp - L), (0, 0))).reshape(B * Lp, DN_CONV_DIM)
    first = jnp.pad(conv_buf, ((0, 0), (SUBLANE - (DN_CONV - 1), 0), (0, 0)))
    qkv_hm = dn_conv(x, first, conv_w, B * Lp, tt, nt)
    beta, g = dn_gates(proj, a_log, dt_bias)

    def rows(a):
        a = jnp.pad(a.reshape(B, L, DN_HV), ((0, 0), (0, Lp - L), (0, 0)))
        return a.reshape(B, n, C, DN_HV).transpose(3, 0, 1, 2).reshape(DN_HV, B * n, 1, C)

    o_hm, S = delta_rule(qkv_hm, rows(g), rows(beta), s0, B, n)
    if Lp != L:
        o_hm = o_hm.reshape(DN_HV, B, Lp, DN_DV)[:, :, :L].reshape(DN_HV, B * L, DN_DV)
    return dn_out(o_hm, proj, norm_g, w_out, h), S, new_buf


def _sg_body(y_ref, lng_ref, lnb_ref, wsp_ref, bsp_ref, a_ref, *v_out, width):
    u = y_ref[:, :width]
    v = y_ref[:, width:]
    xc = v - jnp.mean(v, -1, keepdims=True)
    vn = xc * lax.rsqrt(jnp.mean(xc * xc, -1, keepdims=True) + LN_EPS) * lng_ref[...] + lnb_ref[...]
    if v_out:
        v_out[0][...] = vn
    n = wsp_ref.shape[1]
    gw = width // SG_GROUPS
    tril = lax.broadcasted_iota(jnp.int32, (n, n), 0) >= lax.broadcasted_iota(jnp.int32, (n, n), 1)
    for g in range(SG_GROUPS):
        sl = slice(g * gw, (g + 1) * gw)
        wt = jnp.where(tril, wsp_ref[g], 0.0).astype(BF16)
        mixed = _dot(wt, vn[:, sl].astype(BF16)) + bsp_ref[:, g:g + 1]
        a_ref[:, sl] = u[:, sl] * mixed


def sg_mix(y, ln_g, ln_b, w_sp, b_sp, want_v):
    T = y.shape[0]
    W = y.shape[1] // 2
    n = SG_CHUNK
    out_shape = [jax.ShapeDtypeStruct((T, W), F32)]
    out_specs = [pl.BlockSpec((n, W), lambda i: (i, 0))]
    if want_v:
        out_shape.append(jax.ShapeDtypeStruct((T, W), F32))
        out_specs.append(pl.BlockSpec((n, W), lambda i: (i, 0)))
    return pl.pallas_call(
        functools.partial(_sg_body, width=W),
        out_shape=tuple(out_shape),
        grid=(T // n,),
        in_specs=[pl.BlockSpec((n, 2 * W), lambda i: (i, 0)), _resident((1, W)), _resident((1, W)),
                  _resident((SG_GROUPS, n, n)), _resident((n, SG_GROUPS))],
        out_specs=tuple(out_specs),
        compiler_params=_cparams(("parallel",)),
        name="sg_mix",
    )(y, ln_g.reshape(1, W), ln_b.reshape(1, W), w_sp, b_sp.T)


def _nsa_heads_body(p_ref, cos_ref, sin_ref, qg_ref, kg_ref, qn_ref, qr_ref, ksh_ref, vsh_ref, kwh_ref, vwh_ref,
                    kc_ref, vc_ref, kst_ref, vst_ref, kwt_ref, vwt_ref):
    cos, sin = cos_ref[...], sin_ref[...]
    half = NSA_DH // 2

    def head(col0, h):
        return p_ref[:, col0 + h * NSA_DH:col0 + (h + 1) * NSA_DH]

    def rope(xn):
        return xn * cos + jnp.concatenate([xn[:, half:], xn[:, :half]], axis=-1) * sin

    for h in range(NSA_HQ):
        xn = _rms(head(0, h), qg_ref[...])
        qn_ref[h] = xn
        qr_ref[h] = rope(xn)
    col = NSA_QW
    kc_ref[...] = p_ref[:, col:col + NSA_KW]
    vc_ref[...] = p_ref[:, col + NSA_KW:col + 2 * NSA_KW]
    col += 2 * NSA_KW
    for kind, (kh_ref, kt_ref, vh_ref, vt_ref) in enumerate(((ksh_ref, kst_ref, vsh_ref, vst_ref),
                                                             (kwh_ref, kwt_ref, vwh_ref, vwt_ref))):
        parts = []
        for h in range(NSA_HKV):
            r = rope(_rms(head(col, h), kg_ref[kind + 1:kind + 2]))
            kh_ref[h] = r.astype(BF16)
            parts.append(r)
            vh_ref[h] = head(col + NSA_KW, h).astype(BF16)
        kt_ref[...] = jnp.concatenate(parts, axis=-1)
        vt_ref[...] = p_ref[:, col + NSA_KW:col + 2 * NSA_KW]
        col += 2 * NSA_KW


def nsa_heads(proj, cos, sin, q_norm, k_norm):
    T, N = proj.shape
    Tc = cos.shape[0]
    tl = _pick(Tc, (256, 128, 64, 32, 16, 8))
    n_tab = Tc // tl
    tab = pl.BlockSpec((tl, NSA_DH), lambda i: (i % n_tab, 0))
    hm = lambda H: pl.BlockSpec((H, tl, NSA_DH), lambda i: (0, i, 0))
    tok = pl.BlockSpec((tl, NSA_KW), lambda i: (i, 0))
    return pl.pallas_call(
        _nsa_heads_body,
        out_shape=((jax.ShapeDtypeStruct((NSA_HQ, T, NSA_DH), F32),) * 2
                   + (jax.ShapeDtypeStruct((NSA_HKV, T, NSA_DH), BF16),) * 4
                   + (jax.ShapeDtypeStruct((T, NSA_KW), F32),) * 6),
        grid=(T // tl,),
        in_specs=[pl.BlockSpec((tl, N), lambda i: (i, 0)), tab, tab, _resident((1, NSA_DH)), _resident((3, NSA_DH))],
        out_specs=(hm(NSA_HQ),) * 2 + (hm(NSA_HKV),) * 4 + (tok,) * 6,
        compiler_params=_cparams(("parallel",)),
        name="nsa_heads",
    )(proj, cos, sin, q_norm.reshape(1, NSA_DH), k_norm)


def _cmp_proj_body(*refs):
    x_refs, w_ref, o_ref = refs[:-2], refs[-2], refs[-1]
    n = o_ref.shape[0]

    def tap(l):
        x = jnp.concatenate([r[pl.ds(l, n, stride=CMP_STRIDE), :] for r in x_refs], axis=-1)
        return _dot(x.astype(BF16), w_ref[l])

    acc = tap(0)
    for l in range(1, CMP_STRIDE):
        acc = acc + tap(l)
    o_ref[...] = acc


def cmp_proj(x, wtaps):
    rows = x.shape[0]
    segs = rows // CMP_STRIDE
    tm = _pick(segs, (256, 128, 64, 32, 16, 8))
    return pl.pallas_call(
        _cmp_proj_body,
        out_shape=jax.ShapeDtypeStruct((segs, 2 * NSA_KW), F32),
        grid=(segs // tm,),
        in_specs=[pl.BlockSpec((tm * CMP_STRIDE, LANE), lambda i, c=c: (i, c)) for c in range(NSA_KW // LANE)]
                 + [_resident((CMP_STRIDE, NSA_KW, 2 * NSA_KW))],
        out_specs=pl.BlockSpec((tm, 2 * NSA_KW), lambda i: (i, 0)),
        compiler_params=_cparams(("parallel",)),
        name="cmp_proj",
    )(*([x] * (NSA_KW // LANE)), wtaps)


def _page_copies(pool_ref, buf_ref, sem_ref, page, slot, p):
    if buf_ref.ndim == pool_ref.ndim + 1:
        return [pltpu.make_async_copy(pool_ref.at[page], buf_ref.at[slot, p], sem_ref.at[slot])]
    return [pltpu.make_async_copy(pool_ref.at[page, :, pl.ds(t * LANE, LANE)], buf_ref.at[t, slot, p], sem_ref.at[slot])
            for t in range(buf_ref.shape[0])]


def _fetch_pages(pt_ref, pool_ref, buf_ref, sem_ref, b, chunk, slot, pch):
    for p in range(pch):
        for cp in _page_copies(pool_ref, buf_ref, sem_ref, pt_ref[b, chunk * pch + p], slot, p):
            cp.start()


def _wait_pages(pool_ref, buf_ref, sem_ref, slot, pch):
    for p in range(pch):
        for cp in _page_copies(pool_ref, buf_ref, sem_ref, 0, slot, p):
            cp.wait()


def _cmp_proj_paged_body(pt_ref, pool_ref, w_ref, o_ref, buf_ref, sem_ref, tok_ref, *, pch):
    b, ch = pl.program_id(0), pl.program_id(1)
    nb, nch = pl.num_programs(0), pl.num_programs(1)
    step = b * nch + ch
    slot = lax.rem(step, 2)

    @pl.when(step == 0)
    def _():
        _fetch_pages(pt_ref, pool_ref, buf_ref, sem_ref, b, ch, slot, pch)

    @pl.when(step + 1 < nb * nch)
    def _():
        nxt = step + 1
        _fetch_pages(pt_ref, pool_ref, buf_ref, sem_ref, nxt // nch, lax.rem(nxt, nch), 1 - slot, pch)

    _wait_pages(pool_ref, buf_ref, sem_ref, slot, pch)
    tiles, _, page, _ = tok_ref.shape
    segs = page // CMP_STRIDE
    for p in range(pch):
        rows = buf_ref[slot, p].reshape(NSA_KW, page).T
        for t in range(tiles):
            tok_ref[t, p] = rows[:, t * LANE:(t + 1) * LANE]

    def tap(l):
        x = jnp.concatenate([tok_ref[t, :, pl.ds(l, segs, stride=CMP_STRIDE), :] for t in range(tiles)],
                            axis=-1)
        return _dot(x.reshape(pch * segs, NSA_KW).astype(BF16), w_ref[l])

    acc = tap(0)
    for l in range(1, CMP_STRIDE):
        acc = acc + tap(l)
    o_ref[...] = acc


def cmp_proj_paged(page_table, pool, wtaps):
    B, n_pages = page_table.shape
    page = pool.shape[3]
    segs = page // CMP_STRIDE
    pch = _pick(n_pages, (32, 16, 8, 4, 2, 1))
    nch = n_pages // pch
    return pl.pallas_call(
        functools.partial(_cmp_proj_paged_body, pch=pch),
        out_shape=jax.ShapeDtypeStruct((B * n_pages * segs, 2 * NSA_KW), F32),
        grid_spec=pltpu.PrefetchScalarGridSpec(
            num_scalar_prefetch=1,
            grid=(B, nch),
            in_specs=[pl.BlockSpec(memory_space=pl.ANY),
                      pl.BlockSpec((CMP_STRIDE, NSA_KW, 2 * NSA_KW), lambda b, c, pt: (0, 0, 0))],
            out_specs=pl.BlockSpec((pch * segs, 2 * NSA_KW), lambda b, c, pt: (b * nch + c, 0)),
            scratch_shapes=[pltpu.VMEM((2, pch, NSA_HKV, NSA_DH, page), F32), pltpu.SemaphoreType.DMA((2,)),
                            pltpu.VMEM((NSA_KW // LANE, pch, page, LANE), F32)],
        ),
        compiler_params=_cparams(("arbitrary", "arbitrary")),
        name="cmp_proj_paged",
    )(page_table, pool, wtaps)


def _cmp_finish_body(a_ref, pe_ref, w_ref, g_ref, o_ref, *, do_norm):
    a = a_ref[0]
    n = a.shape[0]
    pe = _dot(pe_ref[0].astype(BF16), w_ref[0])
    for l in range(1, CMP_STRIDE):
        pe = pe + _dot(pe_ref[l].astype(BF16), w_ref[l])
    pe_term = pe[0:1, :NSA_KW] + pe[1:2, NSA_KW:]
    blk = a[:, :NSA_KW] + pltpu.roll(a[:, NSA_KW:], n - 1, 0) + pe_term
    for h in range(NSA_HKV):
        bh = blk[:, h * NSA_DH:(h + 1) * NSA_DH]
        if do_norm:
            bh = _rms(bh, g_ref[...])
        o_ref[0, h] = bh


def cmp_finish(a, pe_rows, wtaps, g, B, n, do_norm):
    return pl.pallas_call(
        functools.partial(_cmp_finish_body, do_norm=do_norm),
        out_shape=jax.ShapeDtypeStruct((B, NSA_HKV, n, NSA_DH), F32),
        grid=(B,),
        in_specs=[pl.BlockSpec((1, n, 2 * NSA_KW), lambda b: (b, 0, 0)), _resident((CMP_STRIDE, SUBLANE, NSA_KW)),
                  _resident((CMP_STRIDE, NSA_KW, 2 * NSA_KW)), _resident((1, NSA_DH))],
        out_specs=pl.BlockSpec((1, NSA_HKV, n, NSA_DH), lambda b: (b, 0, 0, 0)),
        compiler_params=_cparams(("parallel",)),
        name="cmp_finish",
    )(a.reshape(B, n, 2 * NSA_KW), pe_rows, wtaps, g.reshape(1, NSA_DH))


def _masked_softmax(s, mask):
    sm = jnp.where(mask, s, NEG)
    m = jnp.max(sm, -1, keepdims=True)
    p = jnp.where(mask, jnp.exp(s - m), 0.0)
    d = jnp.sum(p, -1, keepdims=True)
    return p / jnp.where(d > 0, d, 1.0)


def _topk_rounds(work, sel, k):
    nl = work.shape[-1]
    lane = lax.broadcasted_iota(jnp.int32, work.shape, 1).astype(F32)
    for _ in range(k):
        m = jnp.max(work, -1, keepdims=True)
        idx = jnp.min(jnp.where(work == m, lane, float(nl)), -1, keepdims=True)
        hit = lane == idx
        sel = jnp.where(hit, 1.0, sel)
        work = jnp.where(hit, -jnp.inf, work)
    return work, sel


SEL_ROUNDS = N_SEL - 3


def _select_start(imp, qpos, n_lanes_valid):
    blk = lax.broadcasted_iota(jnp.int32, imp.shape, 1)
    cur = qpos // SLC_BLOCK
    eligible = (blk * SLC_BLOCK <= qpos) & (blk < n_lanes_valid)
    forced = (blk == 0) | (blk == cur) | (blk == cur - 1)
    assert NSA_G < FORCE_BONUS
    return jnp.where(eligible & ~forced, imp, -jnp.inf), jnp.where(eligible & forced, 1.0, 0.0)


def _select_blocks(imp, qpos, n_lanes_valid):
    return _topk_rounds(*_select_start(imp, qpos, n_lanes_valid), SEL_ROUNDS)[1]


def _flash_tile(q3, k, v, mask, carry):
    return _flash_tile_bias(q3, k, v, jnp.where(mask, 0.0, NEG), carry)


def _flash_tile_bias(q3, k, v, bias, carry):
    m, l, acc = carry
    G, R, dh = q3.shape
    tk = k.shape[0]
    s = _dot_nt(q3.reshape(G * R, dh), k).reshape(G, R, tk) + bias[None]
    m_new = jnp.maximum(m, jnp.max(s, -1, keepdims=True))
    alpha = jnp.exp2(m - m_new)
    p = jnp.exp2(s - m_new)
    l = alpha * l + jnp.sum(p, -1, keepdims=True)
    pv = _dot(p.reshape(G * R, tk).astype(BF16), v).reshape(G, R, dh)
    return m_new, l, alpha * acc + pv


def _flash_tile_t(q3, k_t, v_t, bias, carry):
    m, l, acc = carry
    G, R, dh = q3.shape
    tk = k_t.shape[1]
    s = _dot(q3.reshape(G * R, dh), k_t).reshape(G, R, tk) + bias[None]
    m_new = jnp.maximum(m, jnp.max(s, -1, keepdims=True))
    alpha = jnp.exp2(m - m_new)
    p = jnp.exp2(s - m_new)
    l = alpha * l + jnp.sum(p, -1, keepdims=True)
    pv = _dot_nt(p.reshape(G * R, tk).astype(BF16), v_t).reshape(G, R, dh)
    return m_new, l, alpha * acc + pv


def _flash_init(G, R, dh):
    return (jnp.full((G, R, 1), NEG, F32), jnp.zeros((G, R, 1), F32), jnp.zeros((G, R, dh), F32))


def _flash_done(carry):
    m, l, acc = carry
    return jnp.where(m > 0.5 * NEG, acc / l, 0.0)


def _cmp_branch(qn, kc, vc, overlap, qpos, n_slc):
    G, qb, dh = qn.shape
    ncp = kc.shape[0]
    s = _dot3_nt(qn.reshape(G * qb, dh) * NSA_DH ** -0.5, kc).reshape(G, qb, ncp)
    kend = lax.broadcasted_iota(jnp.int32, (1, ncp), 1) * CMP_STRIDE + (CMP_BLOCK - 1)
    p = _masked_softmax(s, (kend <= qpos)[None])
    o_cmp = _dot(p.reshape(G * qb, ncp).astype(BF16), vc.astype(BF16)).reshape(G, qb, dh)
    psum = p[0]
    for g in range(1, G):
        psum = psum + p[g]
    ps_hi, ps_lo = _split_bf16(psum)
    sel = _select_blocks(_dot(ps_hi, overlap) + _dot(ps_lo, overlap), qpos, n_slc)
    return o_cmp, jnp.where(sel > 0.5, 0.0, NEG)


def _nsa_attn_body(qn_ref, qnn_ref, q_ref, kc_ref, vc_ref, m_ref, ks_ref, vs_ref, kw_ref, vw_ref, gt_ref, o_ref,
                   ocmp_s, selb_s, *, tk, n_slc):
    i = pl.program_id(2)
    qb = q_ref.shape[1]
    nb = m_ref.shape[1]
    row = lax.broadcasted_iota(jnp.int32, (qb, 1), 0)
    qpos = i * qb + row
    kc, vc = kc_ref[0, 0], vc_ref[0, 0]
    overlap = m_ref[...].astype(BF16)

    @pl.when(i == 0)
    def _():
        ocmp_s[...], selb_s[...] = _cmp_branch(qn_ref[...], kc, vc, overlap, row, n_slc)

    o_cmp, sel_bias = ocmp_s[...], selb_s[...]
    ocmp_next, selb_next = _cmp_branch(qnn_ref[...], kc, vc, overlap, qpos + qb, n_slc)

    q3 = (q_ref[...] * QK_SCALE_LOG2).astype(BF16)
    slab = WINDOW + qb
    start = pl.multiple_of(jnp.maximum(i * qb - WINDOW, 0), qb)
    d = qpos - (start + lax.broadcasted_iota(jnp.int32, (1, slab), 1))
    o_swa = _flash_done(_flash_tile(q3, kw_ref[0, pl.ds(start, slab), :], vw_ref[0, pl.ds(start, slab), :],
                                    (d >= 0) & (d <= WINDOW), _flash_init(NSA_G, qb, NSA_DH)))

    bpt = tk // SLC_BLOCK
    expand = jnp.where(lax.broadcasted_iota(jnp.int32, (nb, tk), 0)
                       == lax.broadcasted_iota(jnp.int32, (nb, tk), 1) // SLC_BLOCK, 1.0, 0.0).astype(BF16)

    def key_bias(j):
        rolled = pltpu.roll(sel_bias, lax.rem(nb - j * bpt, nb), 1)
        return _dot(rolled.astype(BF16), expand)

    def slc_step(j, carry):
        start = pl.multiple_of(j * tk, tk)
        return _flash_tile_bias(q3, ks_ref[0, pl.ds(start, tk), :], vs_ref[0, pl.ds(start, tk), :],
                                key_bias(j), carry)

    assert tk % qb == 0
    n_full = (i * qb) // tk
    start = pl.multiple_of(n_full * tk, tk)
    kpos = n_full * tk + lax.broadcasted_iota(jnp.int32, (1, tk), 1)
    diag_bias = key_bias(n_full) + jnp.where(kpos <= qpos, 0.0, NEG)
    carry = _flash_tile_bias(q3, ks_ref[0, pl.ds(start, tk), :], vs_ref[0, pl.ds(start, tk), :], diag_bias,
                             _flash_init(NSA_G, qb, NSA_DH))
    ocmp_s[...], selb_s[...] = ocmp_next, selb_next
    o_slc = _flash_done(lax.fori_loop(0, n_full, slc_step, carry))

    gt = jax.nn.sigmoid(gt_ref[0, 0])
    parts = []
    for g in range(NSA_G):
        parts.append(gt[:, 3 * g:3 * g + 1] * o_cmp[g] + gt[:, 3 * g + 1:3 * g + 2] * o_slc[g]
                     + gt[:, 3 * g + 2:3 * g + 3] * o_swa[g])
    o_ref[0] = jnp.concatenate(parts, axis=-1)


def nsa_attn(qn, qr, kcmp, vcmp, m_mat, ks, vs, kw, vw, gates, n_slc):
    B, _, ncp, dh = kcmp.shape
    L = qr.shape[1] // B
    nb = m_mat.shape[1]
    qb = Q_BLOCK
    tk = _pick(L, (1024, 512, 256, 128))
    assert L >= WINDOW + qb
    nblk = L // qb
    q_spec = pl.BlockSpec((NSA_G, qb, dh), lambda b, h, i: (h, b * nblk + i, 0))
    q_next = pl.BlockSpec((NSA_G, qb, dh), lambda b, h, i: (h, b * nblk + jnp.minimum(i + 1, nblk - 1), 0))
    cmp_spec = pl.BlockSpec((1, 1, ncp, dh), lambda b, h, i: (b, h, 0, 0))
    kv_spec = pl.BlockSpec((1, L, dh), lambda b, h, i: (h, b, 0))
    return pl.pallas_call(
        functools.partial(_nsa_attn_body, tk=tk, n_slc=n_slc),
        out_shape=jax.ShapeDtypeStruct((B, L, NSA_QW), F32),
        grid=(B, NSA_HKV, nblk),
        in_specs=[q_spec, q_next, q_spec, cmp_spec, cmp_spec, _resident((ncp, nb)), kv_spec, kv_spec, kv_spec, kv_spec,
                  pl.BlockSpec((1, 1, qb, 3 * NSA_G), lambda b, h, i: (b, h, i, 0))],
        out_specs=pl.BlockSpec((1, qb, NSA_G * dh), lambda b, h, i: (b, i, h)),
        scratch_shapes=[pltpu.VMEM((NSA_G, qb, dh), F32), pltpu.VMEM((qb, nb), F32)],
        compiler_params=_cparams(("parallel", "parallel", "arbitrary")),
        name="nsa_attn",
    )(qn, qn, qr, kcmp, vcmp, m_mat, ks, vs, kw, vw, gates)


def _sample_attn_body(pt_ref, qn_ref, qr_ref, kc_ref, vc_ref, kpool_ref, vpool_ref, ksn_ref, vsn_ref,
                      bk_ref, bv_ref, kwn_ref, vwn_ref, gt_ref, m_ref, o_ref,
                      kbuf, vbuf, ksem, vsem, sel_s, m_s, l_s, acc_s, *, pch, L, pos0, n_slc):
    b, ch = pl.program_id(0), pl.program_id(1)
    nb, nch = pl.num_programs(0), pl.num_programs(1)
    step = b * nch + ch
    slot = lax.rem(step, 2)
    R = NSA_G * L
    page = kbuf.shape[4]
    tk = pch * page

    @pl.when(step == 0)
    def _():
        _fetch_pages(pt_ref, kpool_ref, kbuf, ksem, b, ch, slot, pch)
        _fetch_pages(pt_ref, vpool_ref, vbuf, vsem, b, ch, slot, pch)

    @pl.when(step + 1 < nb * nch)
    def _():
        nxt = step + 1
        _fetch_pages(pt_ref, kpool_ref, kbuf, ksem, nxt // nch, lax.rem(nxt, nch), 1 - slot, pch)
        _fetch_pages(pt_ref, vpool_ref, vbuf, vsem, nxt // nch, lax.rem(nxt, nch), 1 - slot, pch)

    row = lax.broadcasted_iota(jnp.int32, (R, 1), 0)
    tq = row & (L - 1)
    qpos = pos0 + tq
    same_t = jnp.where((lax.broadcasted_iota(jnp.int32, (R, R), 0) & (L - 1))
                       == (lax.broadcasted_iota(jnp.int32, (R, R), 1) & (L - 1)), 1.0, 0.0)
    scale = NSA_DH ** -0.5

    @pl.when(ch == 0)
    def _():
        for h in range(NSA_HKV):
            kc = kc_ref[0, h]
            ncp = kc.shape[0]
            s = _dot_nt(qn_ref[0, h] * scale, kc, HIGHEST)
            kend = lax.broadcasted_iota(jnp.int32, (1, ncp), 1) * CMP_STRIDE + (CMP_BLOCK - 1)
            p = _masked_softmax(s, kend <= qpos)
            acc_s[0, h] = _dot(p.astype(BF16), vc_ref[0, h].astype(BF16))
            imp = _dotf(_dotf(same_t, p), m_ref[...])
            sel_s[h] = _select_blocks(imp, qpos, n_slc)
            m_s[h] = jnp.full((R, 1), NEG, F32)
            l_s[h] = jnp.zeros((R, 1), F32)
            acc_s[1, h] = jnp.zeros((R, NSA_DH), F32)

    _wait_pages(kpool_ref, kbuf, ksem, slot, pch)
    _wait_pages(vpool_ref, vbuf, vsem, slot, pch)
    nbl = sel_s.shape[2]
    bpc = tk // SLC_BLOCK
    assert bpc <= LANE and nbl % LANE == 0
    expand = jnp.where(lax.broadcasted_iota(jnp.int32, (LANE, tk), 0)
                       == lax.broadcasted_iota(jnp.int32, (LANE, tk), 1) // SLC_BLOCK, 1.0, 0.0).astype(BF16)
    sel_bias = jnp.where(sel_s[...].reshape(NSA_HKV * R, nbl) > 0.5, 0.0, NEG)
    rolled = pltpu.roll(sel_bias, lax.rem(nbl - ch * bpc, nbl), 1)[:, :LANE]
    key_bias = _dot(rolled.astype(BF16), expand)
    for h in range(NSA_HKV):
        q3 = (qr_ref[0, h] * QK_SCALE_LOG2).astype(BF16)[None]
        carry = (m_s[h][None], l_s[h][None], acc_s[1, h][None])
        k_t = jnp.concatenate([kbuf[slot, p, h] for p in range(pch)], axis=-1).astype(BF16)
        v_t = jnp.concatenate([vbuf[slot, p, h] for p in range(pch)], axis=-1).astype(BF16)
        m, l, acc = _flash_tile_t(q3, k_t, v_t, key_bias[h * R:(h + 1) * R], carry)
        m_s[h], l_s[h], acc_s[1, h] = m[0], l[0], acc[0]

    @pl.when(ch == nch - 1)
    def _():
        npad = ksn_ref.shape[1]
        tn = lax.broadcasted_iota(jnp.int32, (1, npad), 1)
        new_ok = (tn <= tq) & (tn < L)
        wb = bk_ref.shape[3]
        wpos = pos0 - wb + lax.broadcasted_iota(jnp.int32, (1, wb), 1)
        dw = qpos - wpos
        win_ok = (dw >= 0) & (dw <= WINDOW)
        cur_blk = pos0 // SLC_BLOCK
        for h in range(NSA_HKV):
            sl = slice(h * NSA_DH, (h + 1) * NSA_DH)
            q3 = (qr_ref[0, h] * QK_SCALE_LOG2).astype(BF16)[None]
            sel_new = sel_s[h][:, cur_blk:cur_blk + 1] > 0.5
            carry = (m_s[h][None], l_s[h][None], acc_s[1, h][None])
            carry = _flash_tile(q3, ksn_ref[0][:, sl].astype(BF16), vsn_ref[0][:, sl].astype(BF16),
                                new_ok & sel_new, carry)
            o_slc = _flash_done(carry)[0]
            carry = _flash_init(1, R, NSA_DH)
            carry = _flash_tile_t(q3, bk_ref[0, h].astype(BF16), bv_ref[0, h].astype(BF16),
                                  jnp.where(win_ok, 0.0, NEG), carry)
            carry = _flash_tile(q3, kwn_ref[0][:, sl].astype(BF16), vwn_ref[0][:, sl].astype(BF16), new_ok, carry)
            o_swa = _flash_done(carry)[0]
            gt = jax.nn.sigmoid(gt_ref[0, h])
            o_ref[0, h] = gt[:, 0:1] * acc_s[0, h] + gt[:, 1:2] * o_slc + gt[:, 2:3] * o_swa


def sample_attn(page_table, qn, qr, kcmp, vcmp, kpool, vpool, ks_new, vs_new, buf_k, buf_v, kw_new, vw_new,
                gates, m_mat, L, pos0, n_slc):
    B, n_pages = page_table.shape
    page = kpool.shape[3]
    wb = buf_k.shape[3]
    R = NSA_G * L
    pch = _pick(n_pages, (16, 8, 4, 2, 1))
    nch = n_pages // pch
    nbl = m_mat.shape[1]
    ncp = kcmp.shape[2]
    assert L & (L - 1) == 0 and pos0 % SLC_BLOCK == 0 and pos0 // SLC_BLOCK < nbl
    per_b = lambda *blk: pl.BlockSpec((1,) + blk, lambda b, c, pt: (b,) + (0,) * len(blk))
    return pl.pallas_call(
        functools.partial(_sample_attn_body, pch=pch, L=L, pos0=pos0, n_slc=n_slc),
        out_shape=jax.ShapeDtypeStruct((B, NSA_HKV, R, NSA_DH), F32),
        grid_spec=pltpu.PrefetchScalarGridSpec(
            num_scalar_prefetch=1,
            grid=(B, nch),
            in_specs=[per_b(NSA_HKV, R, NSA_DH), per_b(NSA_HKV, R, NSA_DH),
                      per_b(NSA_HKV, ncp, NSA_DH), per_b(NSA_HKV, ncp, NSA_DH),
                      pl.BlockSpec(memory_space=pl.ANY), pl.BlockSpec(memory_space=pl.ANY),
                      per_b(SUBLANE, NSA_KW), per_b(SUBLANE, NSA_KW),
                      per_b(NSA_HKV, NSA_DH, wb), per_b(NSA_HKV, NSA_DH, wb),
                      per_b(SUBLANE, NSA_KW), per_b(SUBLANE, NSA_KW),
                      per_b(NSA_HKV, R, 3),
                      pl.BlockSpec((ncp, nbl), lambda b, c, pt: (0, 0))],
            out_specs=per_b(NSA_HKV, R, NSA_DH),
            scratch_shapes=[pltpu.VMEM((2, pch, NSA_HKV, NSA_DH, page), F32),
                            pltpu.VMEM((2, pch, NSA_HKV, NSA_DH, page), F32),
                            pltpu.SemaphoreType.DMA((2,)), pltpu.SemaphoreType.DMA((2,)),
                            pltpu.VMEM((NSA_HKV, R, nbl), F32), pltpu.VMEM((NSA_HKV, R, 1), F32),
                            pltpu.VMEM((NSA_HKV, R, 1), F32), pltpu.VMEM((2, NSA_HKV, R, NSA_DH), F32)],
        ),
        compiler_params=_cparams(("arbitrary", "arbitrary")),
        name="sample_attn",
    )(page_table, qn, qr, kcmp, vcmp, kpool, vpool, ks_new, vs_new, buf_k, buf_v, kw_new, vw_new, gates, m_mat)


def _cmp_to_slc(n_cmp_rows, n_cols, n_slc):
    k = np.arange(n_cmp_rows)[:, None] * CMP_STRIDE
    j = np.arange(n_cols)[None, :] * SLC_BLOCK
    ov = np.minimum(k + CMP_BLOCK, j + SLC_BLOCK) - np.maximum(k, j)
    m = np.clip(ov, 0, None).astype(np.float32) / CMP_BLOCK
    m[:, n_slc:] = 0.0
    return jnp.asarray(m)


def _rope_tables(pos0, L):
    half = NSA_DH // 2
    inv = ROPE_THETA ** (-jnp.arange(half, dtype=F32) / half)
    ang = (pos0 + jnp.arange(L)).astype(F32)[:, None] * inv[None, :]
    cos, sin = jnp.cos(ang), jnp.sin(ang)
    return jnp.concatenate([cos, cos], -1), jnp.concatenate([-sin, sin], -1)


def _cmp_weights(w, pe):
    w = w.reshape(2, CMP_STRIDE, NSA_DH, NSA_DH)
    eye = jnp.eye(NSA_HKV, dtype=F32)
    wtaps = jnp.einsum("alde,hg->lhdage", w, eye).reshape(CMP_STRIDE, NSA_KW, 2 * NSA_KW).astype(BF16)
    pe = pe.reshape(2, CMP_STRIDE, 1, NSA_DH)
    pe_rows = jnp.broadcast_to(pe, (2, CMP_STRIDE, NSA_HKV, NSA_DH)).reshape(2, CMP_STRIDE, NSA_KW)
    return wtaps, jnp.pad(pe_rows.transpose(1, 0, 2), ((0, 0), (0, SUBLANE - 2), (0, 0)))


def _nsa_prompt(h, proj, B, L, q_norm, k_norm, pe_k, pe_v, w_ck, w_cv, w_out):
    cos, sin = _rope_tables(0, L)
    qn, qr, ks_hm, vs_hm, kw_hm, vw_hm, kc, vc, ks, vs, kw, vw = nsa_heads(proj, cos, sin, q_norm, k_norm)
    n_seg = L // CMP_STRIDE
    n_slc = L // SLC_BLOCK
    assert L % Q_BLOCK == 0 and n_slc <= LANE and n_slc >= N_SEL
    wk, pek = _cmp_weights(w_ck, pe_k)
    wv, pev = _cmp_weights(w_cv, pe_v)
    kcmp = cmp_finish(cmp_proj(kc, wk), pek, wk, k_norm[0], B, n_seg, True)
    vcmp = cmp_finish(cmp_proj(vc, wv), pev, wv, k_norm[0], B, n_seg, False)
    gate = proj[:, NSA_QW + 6 * NSA_KW:NSA_QW + 6 * NSA_KW + 3 * NSA_HQ]
    gates = gate.reshape(B, L, NSA_HKV, 3 * NSA_G).transpose(0, 2, 1, 3)
    o = nsa_attn(qn, qr, kcmp, vcmp, _cmp_to_slc(n_seg, LANE, n_slc), ks_hm, vs_hm, kw_hm, vw_hm, gates, n_slc)
    n_keep = min(WINDOW, L)
    tok = lambda a: a.reshape(B, L, NSA_HKV, NSA_DH)
    return (mm_res(o.reshape(B * L, NSA_QW), w_out, h),
            (tok(kc), tok(vc), tok(ks), tok(vs), tok(kw)[:, L - n_keep:], tok(vw)[:, L - n_keep:]))


def _nsa_sample(h, proj, B, L, past, page_table, q_norm, k_norm, pe_k, pe_v, w_ck, w_cv, w_out):
    pc_k, pc_v, ps_k, ps_v, buf_k, buf_v = past
    n_pool, page = pc_k.shape[:2]
    n_pages = page_table.shape[1]
    pos0 = n_pages * page
    cos, sin = _rope_tables(pos0, L)
    qn, qr, _, _, _, _, kc, vc, ks, vs, kw, vw = nsa_heads(proj, jnp.tile(cos, (B, 1)), jnp.tile(sin, (B, 1)),
                                                             q_norm, k_norm)
    gate = proj[:, NSA_QW + 6 * NSA_KW:NSA_QW + 6 * NSA_KW + 3 * NSA_HQ]
    assert L <= CMP_STRIDE and page % CMP_STRIDE == 0 and pos0 % SLC_BLOCK == 0
    n_seg = pos0 // CMP_STRIDE
    n_slc = -(-(pos0 + L) // SLC_BLOCK)
    nbl = _round_up(n_slc, LANE)
    wk, pek = _cmp_weights(w_ck, pe_k)
    wv, pev = _cmp_weights(w_cv, pe_v)
    keys_t = lambda p: p.transpose(0, 2, 3, 1)
    kcmp = cmp_finish(cmp_proj_paged(page_table, keys_t(pc_k), wk), pek, wk, k_norm[0], B, n_seg, True)
    vcmp = cmp_finish(cmp_proj_paged(page_table, keys_t(pc_v), wv), pev, wv, k_norm[0], B, n_seg, False)
    rows = lambda a: (a.reshape(NSA_HKV, NSA_G, B, L, NSA_DH).transpose(2, 0, 1, 3, 4)
                      .reshape(B, NSA_HKV, NSA_G * L, NSA_DH))
    new8 = lambda a: jnp.pad(a.reshape(B, L, NSA_KW), ((0, 0), (0, SUBLANE - L), (0, 0)))
    tok = lambda a: a.reshape(B, L, NSA_HKV, NSA_DH)
    gates = gate.reshape(B, L, NSA_HKV, NSA_G, 3).transpose(0, 2, 3, 1, 4).reshape(B, NSA_HKV, NSA_G * L, 3)
    o = sample_attn(page_table, rows(qn), rows(qr), kcmp, vcmp, keys_t(ps_k), keys_t(ps_v),
                    new8(ks), new8(vs), keys_t(buf_k), keys_t(buf_v),
                    new8(kw), new8(vw), gates, _cmp_to_slc(n_seg, nbl, n_slc), L, pos0, n_slc)
    o = o.reshape(B, NSA_HKV, NSA_G, L, NSA_DH).transpose(0, 3, 1, 2, 4).reshape(B * L, NSA_QW)
    win_k = jnp.concatenate([buf_k, tok(kw)], axis=1)[:, L:]
    win_v = jnp.concatenate([buf_v, tok(vw)], axis=1)[:, L:]
    return mm_res(o, w_out, h), (tok(kc), tok(vc), tok(ks), tok(vs), win_k, win_v)


def _pad_cols(w, n):
    return jnp.pad(w, ((0, 0), (0, n - w.shape[1]))).astype(BF16)


def kernel(x_prompt, x_sample, state_delta, state_conv, cache_swa_k, cache_swa_v, cache_cmp_k, cache_cmp_v, cache_slc_k, cache_slc_v, page_table, norm_mix, norm_ffn, dn_w_in, dn_conv_w, dn_a_log, dn_dt_bias, dn_norm, dn_w_out, sg_w_in, sg_ln_g, sg_ln_b, sg_w_spatial, sg_b_spatial, sg_w_out, nsa_w_in, nsa_q_norm, nsa_k_norm, nsa_cmp_pe_k, nsa_cmp_pe_v, nsa_cmp_w_k, nsa_cmp_w_v, nsa_w_out, ffn_w_gate, ffn_w_up, ffn_w_down):
    Bp, Lp, D = x_prompt.shape
    Bs, Ls, _ = x_sample.shape
    depth = norm_mix.shape[0]
    hp = x_prompt.reshape(Bp * Lp, D)
    hs = x_sample.reshape(Bs * Ls, D)
    dn_S_p, dn_S_s, dn_c_p, dn_c_s, sg_v_s = [], [], [], [], []
    c_p = ([], [], [], [], [], [])
    c_s = ([], [], [], [], [], [])
    for i in range(depth):
        kind, j = i % N_MIXERS, i // N_MIXERS
        if kind == 0:
            w_in = _pad_cols(dn_w_in[j], _round_up(dn_w_in.shape[2], LANE))
            w_out = dn_w_out[j].astype(BF16)
            prm = (dn_conv_w[j], dn_a_log[j], dn_dt_bias[j], dn_norm[j], w_out)
            proj_p = rms_matmul(hp, norm_mix[i], w_in)
            proj_s = rms_matmul(hs, norm_mix[i], w_in)
            zero_buf = jnp.zeros((Bp, DN_CONV - 1, DN_CONV_DIM), F32)
            zero_S = jnp.zeros((Bp, DN_HV, DN_DK, DN_DV), F32)
            hp, Sp, cp = _delta_group(hp, proj_p, Bp, Lp, zero_buf, zero_S, *prm)
            hs, Ss, cs = _delta_group(hs, proj_s, Bs, Ls, state_conv[j], state_delta[j], *prm)
            dn_S_p.append(Sp); dn_S_s.append(Ss); dn_c_p.append(cp); dn_c_s.append(cs)
        elif kind == 1:
            W = sg_w_in.shape[2] // 2
            w_in = sg_w_in[j].astype(BF16)
            w_out = sg_w_out[j].astype(BF16)
            prm = (sg_ln_g[j], sg_ln_b[j], sg_w_spatial[j], sg_b_spatial[j])
            y_p = rms_matmul(hp, norm_mix[i], w_in, act="gelu")
            y_s = rms_matmul(hs, norm_mix[i], w_in, act="gelu")
            (a_p,) = sg_mix(y_p, *prm, want_v=False)
            y_s = jnp.pad(y_s.reshape(Bs, Ls, 2 * W), ((0, 0), (0, SG_CHUNK - Ls), (0, 0)))
            a_s, v_s = sg_mix(y_s.reshape(Bs * SG_CHUNK, 2 * W), *prm, want_v=True)
            unpad = lambda a: a.reshape(Bs, SG_CHUNK, W)[:, :Ls]
            hp = mm_res(a_p, w_out, hp)
            hs = mm_res(unpad(a_s).reshape(Bs * Ls, W), w_out, hs)
            sg_v_s.append(unpad(v_s))
        else:
            w_in = _pad_cols(nsa_w_in[j], _round_up(nsa_w_in.shape[2], LANE))
            w_out = nsa_w_out[j].astype(BF16)
            prm = (nsa_q_norm[j], nsa_k_norm[j], nsa_cmp_pe_k[j], nsa_cmp_pe_v[j], nsa_cmp_w_k[j], nsa_cmp_w_v[j], w_out)
            past = (cache_cmp_k[j], cache_cmp_v[j], cache_slc_k[j], cache_slc_v[j], cache_swa_k[j], cache_swa_v[j])
            proj_p = rms_matmul(hp, norm_mix[i], w_in)
            proj_s = rms_matmul(hs, norm_mix[i], w_in)
            hp, new_p = _nsa_prompt(hp, proj_p, Bp, Lp, *prm)
            hs, new_s = _nsa_sample(hs, proj_s, Bs, Ls, past, page_table, *prm)
            for lst, a in zip(c_p, new_p):
                lst.append(a)
            for lst, a in zip(c_s, new_s):
                lst.append(a)
        wg, wu, wd = ffn_w_gate[i].astype(BF16), ffn_w_up[i].astype(BF16), ffn_w_down[i].astype(BF16)
        hp = ffn(hp, norm_ffn[i], wg, wu, wd)
        hs = ffn(hs, norm_ffn[i], wg, wu, wd)
    return (hp.reshape(Bp, Lp, D), hs.reshape(Bs, Ls, D),
            jnp.stack(dn_S_p), jnp.stack(dn_S_s), jnp.stack(dn_c_p), jnp.stack(dn_c_s),
            jnp.stack(sg_v_s),
            jnp.stack(c_p[4]), jnp.stack(c_p[5]), jnp.stack(c_s[4]), jnp.stack(c_s[5]),
            jnp.stack(c_p[0]), jnp.stack(c_p[1]), jnp.stack(c_p[2]), jnp.stack(c_p[3]),
            jnp.stack(c_s[0]), jnp.stack(c_s[1]), jnp.stack(c_s[2]), jnp.stack(c_s[3]))
```
